```python
import jax, jax.numpy as jnp
from jax import lax
import numpy as np

D_MODEL = 1024
BATCH = 16
SEQ = 4096
DEPTH = 4
DEC_BATCH = 32
DEC_SEQ = 16
PAST_LEN = 4096

CHUNK = 64
N_MEM = 256
EPS = 1e-6
GDN_HEADS = 4
GDN_DK = 128
GDN_DV = 128
GDN_CONV = 4
GDN_QK = GDN_HEADS * GDN_DK
GDN_V = GDN_HEADS * GDN_DV
GDN_CONV_DIM = 2 * GDN_QK + GDN_V
SC_DIM = 256
SC_GROUPS = 4
SC_CONV = 3
SB_HEADS = 4
SB_DH = 64
SB_W = SB_HEADS * SB_DH
SB_BLOCK = 128
D_MIX = GDN_V + SC_DIM + SB_W
IN_SIZES = (GDN_QK, GDN_QK, GDN_V, GDN_V, GDN_HEADS, GDN_HEADS,
            SC_DIM, SC_DIM, SC_DIM, SB_W, SB_W, SB_W)
D_IN = 2 * GDN_QK + 2 * GDN_V + 2 * GDN_HEADS + 3 * SC_DIM + 3 * SB_W
MEM_HEADS = 4
MEM_DH = 128
MEM_W = MEM_HEADS * MEM_DH
D_FF = 2816
FFN_CONV = 3

kernel_name = "hybrid_streaming_encoder_step"

F32 = jnp.float32


def rmsnorm(x, g):
    xf = x.astype(F32)
    y = xf * lax.rsqrt(jnp.mean(xf * xf, axis=-1, keepdims=True) + EPS)
    return (y * g.astype(F32)).astype(x.dtype)


def group_rmsnorm(y, n_groups, g):
    B, T, C = y.shape
    yf = y.astype(F32).reshape(B, T, n_groups, C // n_groups)
    yf = yf * lax.rsqrt(jnp.mean(yf * yf, axis=-1, keepdims=True) + EPS)
    return yf.reshape(B, T, C) * g.astype(F32)


def l2norm(x):
    return x * lax.rsqrt(jnp.sum(x * x, axis=-1, keepdims=True) + EPS)


def causal_dwconv(x, buf, w):
    W = w.shape[0]
    T = x.shape[1]
    xp = jnp.concatenate([buf.astype(x.dtype), x], axis=1)
    y = xp[:, 0:T] * w[0]
    for i in range(1, W):
        y = y + xp[:, i:i + T] * w[i]
    return y, xp[:, T:]


def split_cols(p):
    out, start = [], 0
    for s in IN_SIZES:
        out.append(p[..., start:start + s])
        start += s
    return out


def to_heads(a, d):
    B, T, _ = a.shape
    return a.reshape(B, T, -1, d).transpose(0, 2, 1, 3)


def gdn_chunk(S, q, k, v, g, beta):
    T = q.shape[2]
    gc = jnp.cumsum(g, axis=-1)
    idx = jnp.arange(T)
    causal = idx[:, None] >= idx[None, :]
    strict = idx[:, None] > idx[None, :]
    diff = gc[..., :, None] - gc[..., None, :]
    decay = jnp.exp(jnp.where(causal, diff, -jnp.inf))
    kk = jnp.einsum('bhtd,bhsd->bhts', k, k)
    L = jnp.where(strict, beta[..., None] * kk * decay, 0.0)
    A = L + jnp.eye(T, dtype=L.dtype)
    rhs = jnp.concatenate([v * beta[..., None],
                           k * (beta * jnp.exp(gc))[..., None]], axis=-1)
    sol = lax.linalg.triangular_solve(A, rhs, left_side=True, lower=True,
                                      unit_diagonal=True)
    u, w = sol[..., :GDN_DV], sol[..., GDN_DV:]
    v_new = u - jnp.einsum('bhtk,bhkv->bhtv', w, S)
    qk = jnp.einsum('bhtd,bhsd->bhts', q, k) * decay
    o = (jnp.einsum('bhtk,bhkv->bhtv', q * jnp.exp(gc)[..., None], S)
         + jnp.einsum('bhts,bhsv->bhtv', qk, v_new))
    g_last = gc[..., -1]
    S_new = (S * jnp.exp(g_last)[..., None, None]
             + jnp.einsum('bhtk,bhtv->bhkv',
                          k * jnp.exp(g_last[..., None] - gc)[..., None], v_new))
    return S_new, o


def gdn_prompt(q, k, v, g, beta):
    B, H, T, _ = q.shape
    nc = T // CHUNK

    def split(a):
        return jnp.moveaxis(a.reshape(a.shape[:2] + (nc, CHUNK) + a.shape[3:]), 2, 0)

    S0 = jnp.zeros((B, H, GDN_DK, GDN_DV), F32)
    S, o = lax.scan(lambda S, xs: gdn_chunk(S, *xs), S0,
                    (split(q), split(k), split(v), split(g), split(beta)))
    o = jnp.moveaxis(o, 0, 2).reshape(B, H, T, GDN_DV)
    return o, S


def stick_breaking(q, k, v, q_pos, k_pos):
    z = jnp.einsum('bhtd,bhsd->bhts', q.astype(F32), k.astype(F32)) * (SB_DH ** -0.5)
    strict = k_pos[None, :] < q_pos[:, None]
    log_1mb = jnp.where(strict, jax.nn.log_sigmoid(-z), 0.0)
    rc = lax.cumsum(log_1mb, axis=3, reverse=True) - log_1mb
    logA = jnp.where(strict, jax.nn.log_sigmoid(z) + rc, -jnp.inf)
    return jnp.einsum('bhts,bhsd->bhtd', jnp.exp(logA), v.astype(F32))


def sb_prompt(q, k, v):
    B, H, T, D = q.shape
    nb = T // SB_BLOCK
    qb = jnp.moveaxis(q.reshape(B, H, nb, SB_BLOCK, D), 2, 0)
    pos_b = jnp.arange(T, dtype=jnp.int32).reshape(nb, SB_BLOCK)
    k_pos = jnp.arange(T, dtype=jnp.int32)
    o = lax.map(lambda xs: stick_breaking(xs[0], k, v, xs[1], k_pos), (qb, pos_b))
    return jnp.moveaxis(o, 0, 2).reshape(B, H, T, D)


def memory_kv(mem, mem_in_norm_g, w_mk, w_mv, mk_norm_g):
    B, N, _ = mem.shape
    m = rmsnorm(mem, mem_in_norm_g)
    k = rmsnorm((m @ w_mk).reshape(B, N, MEM_HEADS, MEM_DH), mk_norm_g)
    v = (m @ w_mv).reshape(B, N, MEM_HEADS, MEM_DH)
    return k, v


def hybrid_layer(x, mem_k, mem_v, gdn_S, gdn_buf, sc_buf, sb_past_k, sb_past_v, ffn_buf, lw):
    (norm_mix_g, w_in, gdn_conv_w, gdn_A_log, gdn_dt_bias, gdn_norm_g,
     sc_conv_w, sc_norm_g, sb_norm_g, w_mix_out,
     norm_mem_g, w_mq, mq_norm_g, w_mo,
     norm_ffn_g, w_gate, w_up, ffn_conv_w, w_down) = lw
    B, T, _ = x.shape
    dt = x.dtype

    h = rmsnorm(x, norm_mix_g)
    gq, gk, gv, gz, gb, ga, sB, sC, sx, cq, ck, cv = split_cols(h @ w_in)

    qkv, gdn_buf_new = causal_dwconv(jnp.concatenate([gq, gk, gv], axis=-1), gdn_buf, gdn_conv_w)
    qkv = jax.nn.silu(qkv.astype(F32))
    q = l2norm(to_heads(qkv[..., :GDN_QK], GDN_DK)) * (GDN_DK ** -0.5)
    k = l2norm(to_heads(qkv[..., GDN_QK:2 * GDN_QK], GDN_DK))
    v = to_heads(qkv[..., 2 * GDN_QK:], GDN_DV)
    beta = jax.nn.sigmoid(gb.astype(F32)).transpose(0, 2, 1)
    g = (-jnp.exp(gdn_A_log.astype(F32))
         * jax.nn.softplus(ga.astype(F32) + gdn_dt_bias.astype(F32))).transpose(0, 2, 1)
    if gdn_S is None:
        o_a, S_new = gdn_prompt(q, k, v, g, beta)
    else:
        S_new, o_a = gdn_chunk(gdn_S.astype(F32), q, k, v, g, beta)
    o_a = o_a.transpose(0, 2, 1, 3)
    zh = gz.reshape(B, T, GDN_HEADS, GDN_DV).astype(F32)
    y_a = (rmsnorm(o_a, gdn_norm_g) * jax.nn.silu(zh)).reshape(B, T, GDN_V).astype(dt)

    u, sc_buf_new = causal_dwconv(sC * sx, sc_buf, sc_conv_w)
    y_b = group_rmsnorm(sB * u, SC_GROUPS, sc_norm_g).astype(dt)

    qc, kc, vc = to_heads(cq, SB_DH), to_heads(ck, SB_DH), to_heads(cv, SB_DH)
    if sb_past_k is None:
        o_c = sb_prompt(qc, kc, vc)
    else:
        P = sb_past_k.shape[1]
        k_all = jnp.concatenate([sb_past_k.astype(dt).transpose(0, 2, 1, 3), kc], axis=2)
        v_all = jnp.concatenate([sb_past_v.astype(dt).transpose(0, 2, 1, 3), vc], axis=2)
        q_pos = P + jnp.arange(T, dtype=jnp.int32)
        k_pos = jnp.arange(P + T, dtype=jnp.int32)
        o_c = stick_breaking(qc, k_all, v_all, q_pos, k_pos)
    y_c = group_rmsnorm(o_c.transpose(0, 2, 1, 3).reshape(B, T, SB_W), SB_HEADS, sb_norm_g).astype(dt)
    sb_k_new = ck.reshape(B, T, SB_HEADS, SB_DH)
    sb_v_new = cv.reshape(B, T, SB_HEADS, SB_DH)

    x = x + jnp.concatenate([y_a, y_b, y_c], axis=-1) @ w_mix_out

    h = rmsnorm(x, norm_mem_g)
    qm = rmsnorm((h @ w_mq).reshape(B, T, MEM_HEADS, MEM_DH), mq_norm_g)
    s = jnp.einsum('bthd,bnhd->bhtn', qm.astype(F32), mem_k.astype(F32)) * (MEM_DH ** -0.5)
    p = jax.nn.softmax(s, axis=-1)
    om = jnp.einsum('bhtn,bnhd->bthd', p, mem_v.astype(F32)).reshape(B, T, MEM_W).astype(dt)
    x = x + om @ w_mo

    h = rmsnorm(x, norm_ffn_g)
    gt, ffn_buf_new = causal_dwconv(h @ w_gate, ffn_buf, ffn_conv_w)
    x = x + (jax.nn.silu(gt) * (h @ w_up)) @ w_down

    return x, S_new, gdn_buf_new, sc_buf_new, sb_k_new, sb_v_new, ffn_buf_new


def setup_inputs(seed: int = 0) -> dict:
    key = jax.random.key(seed)
    ks = iter(jax.random.split(key, 48))

    def nrm(shape, scale=1.0):
        return jax.random.normal(next(ks), shape, F32) * scale

    def gain(shape):
        return 1.0 + 0.02 * jax.random.normal(next(ks), shape, F32)

    dt_init = jax.random.uniform(next(ks), (DEPTH, GDN_HEADS), F32, 0.001, 0.1)
    return {
        "x_prompt": nrm((BATCH, SEQ, D_MODEL)),
        "x_sample": nrm((DEC_BATCH, DEC_SEQ, D_MODEL)),
        "mem_prompt": nrm((BATCH, N_MEM, D_MODEL)),
        "state_gdn": nrm((DEPTH, DEC_BATCH, GDN_HEADS, GDN_DK, GDN_DV), 0.05),
        "cache_gdn_conv": nrm((DEPTH, DEC_BATCH, GDN_CONV - 1, GDN_CONV_DIM)),
        "cache_sc_conv": nrm((DEPTH, DEC_BATCH, SC_CONV - 1, SC_DIM)),
        "cache_sb_k": nrm((DEPTH, DEC_BATCH, PAST_LEN, SB_HEADS, SB_DH)),
        "cache_sb_v": nrm((DEPTH, DEC_BATCH, PAST_LEN, SB_HEADS, SB_DH)),
        "cache_mem_k": nrm((DEPTH, DEC_BATCH, N_MEM, MEM_HEADS, MEM_DH)),
        "cache_mem_v": nrm((DEPTH, DEC_BATCH, N_MEM, MEM_HEADS, MEM_DH)),
        "cache_ffn_conv": nrm((DEPTH, DEC_BATCH, FFN_CONV - 1, D_FF)),
        "norm_mix_g": gain((DEPTH, D_MODEL)),
        "w_in": nrm((DEPTH, D_MODEL, D_IN), D_MODEL ** -0.5),
        "gdn_conv_w": nrm((DEPTH, GDN_CONV, GDN_CONV_DIM), GDN_CONV ** -0.5),
        "gdn_A_log": jnp.log(jax.random.uniform(next(ks), (DEPTH, GDN_HEADS), F32, 1.0, 16.0)),
        "gdn_dt_bias": jnp.log(jnp.expm1(dt_init)),
        "gdn_norm_g": gain((DEPTH, GDN_DV)),
        "sc_conv_w": nrm((DEPTH, SC_CONV, SC_DIM), SC_CONV ** -0.5),
        "sc_norm_g": gain((DEPTH, SC_DIM)),
        "sb_norm_g": gain((DEPTH, SB_W)),
        "w_mix_out": nrm((DEPTH, D_MIX, D_MODEL), 0.5 * D_MIX ** -0.5),
        "norm_mem_g": gain((DEPTH, D_MODEL)),
        "mem_in_norm_g": gain((DEPTH, D_MODEL)),
        "w_mq": nrm((DEPTH, D_MODEL, MEM_W), D_MODEL ** -0.5),
        "w_mk": nrm((DEPTH, D_MODEL, MEM_W), D_MODEL ** -0.5),
        "w_mv": nrm((DEPTH, D_MODEL, MEM_W), D_MODEL ** -0.5),
        "mq_norm_g": gain((DEPTH, MEM_DH)),
        "mk_norm_g": gain((DEPTH, MEM_DH)),
        "w_mo": nrm((DEPTH, MEM_W, D_MODEL), 0.5 * MEM_W ** -0.5),
        "norm_ffn_g": gain((DEPTH, D_MODEL)),
        "w_gate": nrm((DEPTH, D_MODEL, D_FF), D_MODEL ** -0.5),
        "w_up": nrm((DEPTH, D_MODEL, D_FF), D_MODEL ** -0.5),
        "ffn_conv_w": nrm((DEPTH, FFN_CONV, D_FF), FFN_CONV ** -0.5),
        "w_down": nrm((DEPTH, D_FF, D_MODEL), 0.5 * D_FF ** -0.5),
    }


def reference(x_prompt, x_sample, mem_prompt, state_gdn, cache_gdn_conv, cache_sc_conv,
              cache_sb_k, cache_sb_v, cache_mem_k, cache_mem_v, cache_ffn_conv,
              norm_mix_g, w_in, gdn_conv_w, gdn_A_log, gdn_dt_bias, gdn_norm_g,
              sc_conv_w, sc_norm_g, sb_norm_g, w_mix_out,
              norm_mem_g, mem_in_norm_g, w_mq, w_mk, w_mv, mq_norm_g, mk_norm_g, w_mo,
              norm_ffn_g, w_gate, w_up, ffn_conv_w, w_down):
    xp, xs = x_prompt, x_sample
    Bp = xp.shape[0]
    P_S, P_GC, P_SC, P_K, P_V, P_MK, P_MV, P_FC = [], [], [], [], [], [], [], []
    S_S, S_GC, S_SC, S_K, S_V, S_FC = [], [], [], [], [], []
    for l in range(DEPTH):
        lw = (norm_mix_g[l], w_in[l], gdn_conv_w[l], gdn_A_log[l], gdn_dt_bias[l], gdn_norm_g[l],
              sc_conv_w[l], sc_norm_g[l], sb_norm_g[l], w_mix_out[l],
              norm_mem_g[l], w_mq[l], mq_norm_g[l], w_mo[l],
              norm_ffn_g[l], w_gate[l], w_up[l], ffn_conv_w[l], w_down[l])
        mk, mv = memory_kv(mem_prompt, mem_in_norm_g[l], w_mk[l], w_mv[l], mk_norm_g[l])
        xp, S_n, gc_n, sc_n, k_n, v_n, fc_n = hybrid_layer(
            xp, mk, mv, None,
            jnp.zeros((Bp, GDN_CONV - 1, GDN_CONV_DIM), xp.dtype),
            jnp.zeros((Bp, SC_CONV - 1, SC_DIM), xp.dtype),
            None, None,
            jnp.zeros((Bp, FFN_CONV - 1, D_FF), xp.dtype), lw)
        P_S.append(S_n); P_GC.append(gc_n); P_SC.append(sc_n); P_K.append(k_n)
        P_V.append(v_n); P_MK.append(mk); P_MV.append(mv); P_FC.append(fc_n)
        xs, S_n, gc_n, sc_n, k_n, v_n, fc_n = hybrid_layer(
            xs, cache_mem_k[l], cache_mem_v[l], state_gdn[l], cache_gdn_conv[l],
            cache_sc_conv[l], cache_sb_k[l], cache_sb_v[l], cache_ffn_conv[l], lw)
        S_S.append(S_n); S_GC.append(gc_n); S_SC.append(sc_n)
        S_K.append(k_n); S_V.append(v_n); S_FC.append(fc_n)
    return (xp, xs,
            jnp.stack(P_S), jnp.stack(P_GC), jnp.stack(P_SC), jnp.stack(P_K), jnp.stack(P_V),
            jnp.stack(P_MK), jnp.stack(P_MV), jnp.stack(P_FC),
            jnp.stack(S_S), jnp.stack(S_GC), jnp.stack(S_SC), jnp.stack(S_K), jnp.stack(S_V),
            jnp.stack(S_FC))
```

```python
import functools

import jax
import jax.numpy as jnp
from jax import lax
from jax.experimental import pallas as pl
from jax.experimental.pallas import tpu as pltpu

F32 = jnp.float32
BF16 = jnp.bfloat16
HI = lax.Precision.HIGHEST
EPS = 1e-6

GDN_HEADS = 4
GDN_DK = 128
GDN_QK = 512
GDN_CONV_DIM = 1536
SC_DIM = 256
SB_HEADS = 4
SB_DH = 64
SB_W = 256
MEM_HEADS = 4
MEM_DH = 128
CHUNK = 64
HIST = 8
FF_CHUNK = 256
SB_LOG_ZERO = -104.0
VMEM_LIMIT = 56 * 1024 * 1024

C_QKV, C_GZ, C_GAB, C_SBCX, C_CQ, C_CK, C_CV, C_END = 0, 1536, 2048, 2176, 2944, 3200, 3456, 3712


def _dot(a, b, precision=None):
    return jnp.dot(a, b, preferred_element_type=F32, precision=precision)


def _dot_nt(a, b, precision=None):
    return lax.dot_general(a, b, (((1,), (1,)), ((), ())),
                           preferred_element_type=F32, precision=precision)


def _dot_tn(a, b, precision=None):
    return lax.dot_general(a, b, (((0,), (0,)), ((), ())),
                           preferred_element_type=F32, precision=precision)


def _rms(x, g):
    return x * lax.rsqrt(jnp.mean(x * x, axis=-1, keepdims=True) + EPS) * g


def _sigmoid(x):
    return 1.0 / (1.0 + jnp.exp(-x))


def _silu(x):
    return x * _sigmoid(x)


def _softplus(x):
    return jnp.maximum(x, 0.0) + jnp.log1p(jnp.exp(-jnp.abs(x)))


def _group_mean_matrix(width, group):
    r = lax.broadcasted_iota(jnp.int32, (width, width), 0) // group
    c = lax.broadcasted_iota(jnp.int32, (width, width), 1) // group
    return jnp.where(r == c, 1.0 / group, 0.0).astype(F32)


def _params(n_axes):
    return pltpu.CompilerParams(dimension_semantics=("arbitrary",) * n_axes,
                                vmem_limit_bytes=VMEM_LIMIT)


def _const_spec(shape):
    nd = len(shape)
    return pl.BlockSpec(shape, lambda *_: (0,) * nd, pipeline_mode=pl.Buffered(1))


def _in_proj_kernel(x_ref, g_ref, w_ref, qkv_ref, gz_ref, gab_ref, sbcx_ref,
                    cq_ref, ck_ref, cv_ref, ckb_ref, cvb_ref):
    h = _rms(x_ref[...], g_ref[...]).astype(BF16)

    def proj(c0, c1):
        return _dot(h, w_ref[:, c0:c1])

    for c in range(0, GDN_CONV_DIM, 512):
        qkv_ref[:, c:c + 512] = proj(C_QKV + c, C_QKV + c + 512)
    gz_ref[...] = proj(C_GZ, C_GAB)
    gab_ref[...] = proj(C_GAB, C_SBCX)
    for c in range(0, 3 * SC_DIM, 256):
        sbcx_ref[:, c:c + 256] = proj(C_SBCX + c, C_SBCX + c + 256)
    cq_ref[...] = (proj(C_CQ, C_CK) * (SB_DH ** -0.5)).astype(BF16)
    k = proj(C_CK, C_CV)
    ck_ref[...] = k
    ckb_ref[...] = k.astype(BF16)
    v = proj(C_CV, C_END)
    cv_ref[...] = v
    cvb_ref[...] = v.astype(BF16)


def _in_proj(x, g, w, tm):
    n, d = x.shape
    row = lambda width: pl.BlockSpec((tm, width), lambda i: (i, 0))
    out_shape = (
        jax.ShapeDtypeStruct((n, GDN_CONV_DIM), F32),
        jax.ShapeDtypeStruct((n, GDN_QK), F32),
        jax.ShapeDtypeStruct((n, 128), F32),
        jax.ShapeDtypeStruct((n, 3 * SC_DIM), F32),
        jax.ShapeDtypeStruct((n, SB_W), BF16),
        jax.ShapeDtypeStruct((n, SB_W), F32),
        jax.ShapeDtypeStruct((n, SB_W), F32),
        jax.ShapeDtypeStruct((n, SB_W), BF16),
        jax.ShapeDtypeStruct((n, SB_W), BF16),
    )
    return pl.pallas_call(
        _in_proj_kernel,
        out_shape=out_shape,
        grid=(n // tm,),
        in_specs=[row(d), _const_spec((1, d)), _const_spec(w.shape)],
        out_specs=(row(GDN_CONV_DIM), row(GDN_QK), row(128), row(3 * SC_DIM),
                   row(SB_W), row(SB_W), row(SB_W), row(SB_W), row(SB_W)),
        compiler_params=_params(1),
        name="in_proj",
    )(x, g, w)


def _gdn_kernel(qkv_ref, gz_ref, gab_ref, sbcx_ref, hg_ref, hs_ref, s0_ref,
                cw_ref, scw_ref, gp_ref, gng_ref, sng_ref,
                ya_ref, yb_ref, s_ref, tg_ref, ts_ref, extg, exts, *, tt, chunk):
    t = pl.program_id(1)

    @pl.when(t == 0)
    def _():
        extg[0:HIST, :] = hg_ref[...]
        exts[0:HIST, :] = hs_ref[...]
        s_ref[...] = s0_ref[...]

    @pl.when(t > 0)
    def _():
        extg[0:HIST, :] = extg[tt:tt + HIST, :]
        exts[0:HIST, :] = exts[tt:tt + HIST, :]

    x = qkv_ref[...]
    extg[HIST:HIST + tt, :] = x
    tg_ref[...] = x[tt - HIST:tt, :]
    conv = (extg[HIST - 3:HIST - 3 + tt, :] * cw_ref[0:1, :]
            + extg[HIST - 2:HIST - 2 + tt, :] * cw_ref[1:2, :]
            + extg[HIST - 1:HIST - 1 + tt, :] * cw_ref[2:3, :]
            + x * cw_ref[3:4, :])
    act = _silu(conv)

    slab = gab_ref[...]
    beta_s = _sigmoid(slab)
    g_s = -jnp.exp(gp_ref[0:1, :]) * _softplus(slab + gp_ref[1:2, :])

    ri = lax.broadcasted_iota(jnp.int32, (chunk, chunk), 0)
    ci = lax.broadcasted_iota(jnp.int32, (chunk, chunk), 1)
    causal = ri >= ci
    strict = ri > ci
    tri_incl = jnp.where(causal, 1.0, 0.0).astype(F32)
    eye = jnp.where(ri == ci, 1.0, 0.0).astype(F32)
    sel = jnp.where(lax.broadcasted_iota(jnp.int32, (8, 128), 1)
                    == lax.broadcasted_iota(jnp.int32, (8, 128), 0) + GDN_HEADS, 1.0, 0.0).astype(F32)
    n_levels = chunk.bit_length() - 2
    gng = gng_ref[...]

    for c in range(tt // chunk):
        r0 = c * chunk
        gc = _dot(tri_incl, g_s[r0:r0 + chunk, :], HI)
        gc_t = _dot_nt(sel, gc, HI)
        eg = jnp.exp(gc)
        g_last = gc[chunk - 1:chunk, :]
        ek = jnp.exp(g_last - gc)
        eg_last = jnp.exp(g_last)
        for h in range(GDN_HEADS):
            col = GDN_HEADS + h
            lanes = slice(GDN_DK * h, GDN_DK * (h + 1))
            diff = gc[:, col:col + 1] - gc_t[h:h + 1, :]
            decay = jnp.exp(jnp.where(causal, diff, -jnp.inf))
            bcol = beta_s[r0:r0 + chunk, h:h + 1]
            qh = act[r0:r0 + chunk, lanes]
            kh = act[r0:r0 + chunk, GDN_QK + GDN_DK * h:GDN_QK + GDN_DK * (h + 1)]
            vh = act[r0:r0 + chunk, 2 * GDN_QK + GDN_DK * h:2 * GDN_QK + GDN_DK * (h + 1)]
            qh = qh * lax.rsqrt(jnp.sum(qh * qh, axis=-1, keepdims=True) + EPS) * (GDN_DK ** -0.5)
            kh = kh * lax.rsqrt(jnp.sum(kh * kh, axis=-1, keepdims=True) + EPS)
            qb = qh.astype(BF16)
            kb = kh.astype(BF16)
            kk = _dot_nt(kb, kb)
            nm = jnp.where(strict, -(bcol * kk * decay), 0.0)
            inv = eye + nm
            pw = nm
            for _ in range(n_levels):
                pw = _dot(pw, pw, HI)
                inv = inv + _dot(inv, pw, HI)
            rhs = jnp.concatenate([vh * bcol, kh * (bcol * eg[:, col:col + 1])], axis=1)
            sol = _dot(inv, rhs, HI)
            u = sol[:, :GDN_DK]
            w = sol[:, GDN_DK:]
            s_old = s_ref[h]
            sb = s_old.astype(BF16)
            v_new = u - _dot(w.astype(BF16), sb)
            vb = v_new.astype(BF16)
            qk = _dot_nt(qb, kb) * decay
            o = _dot((qh * eg[:, col:col + 1]).astype(BF16), sb) + _dot(qk.astype(BF16), vb)
            kdec = (kh * ek[:, col:col + 1]).astype(BF16)
            s_ref[h] = s_old * eg_last[:, col:col + 1] + _dot_tn(kdec, vb)
            zh = gz_ref[r0:r0 + chunk, lanes]
            ya_ref[r0:r0 + chunk, lanes] = (_rms(o, gng) * _silu(zh)).astype(BF16)

    sbcx = sbcx_ref[...]
    s_b = sbcx[:, 0:SC_DIM]
    pre = sbcx[:, SC_DIM:2 * SC_DIM] * sbcx[:, 2 * SC_DIM:3 * SC_DIM]
    exts[HIST:HIST + tt, :] = pre
    ts_ref[...] = pre[tt - HIST:tt, :]
    u_c = (exts[HIST - 2:HIST - 2 + tt, :] * scw_ref[0:1, :]
           + exts[HIST - 1:HIST - 1 + tt, :] * scw_ref[1:2, :]
           + pre * scw_ref[2:3, :])
    yb = s_b * u_c
    ms = _dot(yb * yb, _group_mean_matrix(SC_DIM, SC_DIM // 4), HI)
    yb_ref[...] = (yb * lax.rsqrt(ms + EPS) * sng_ref[...]).astype(BF16)


def _gdn(qkv, gz, gab, sbcx, hist_g, hist_s, s0, cw, scw, gp, gng, sng, batch, tt, chunk):
    n = qkv.shape[0]
    nt = n // batch // tt
    row = lambda width: pl.BlockSpec((tt, width), lambda b, t: (b * nt + t, 0))
    per_b3 = lambda width: pl.BlockSpec((None, HIST, width), lambda b, t: (b, 0, 0))
    state = pl.BlockSpec((None, GDN_HEADS, GDN_DK, GDN_DK), lambda b, t: (b, 0, 0, 0))
    out_shape = (
        jax.ShapeDtypeStruct((n, GDN_QK), BF16),
        jax.ShapeDtypeStruct((n, SC_DIM), BF16),
        jax.ShapeDtypeStruct((batch, GDN_HEADS, GDN_DK, GDN_DK), F32),
        jax.ShapeDtypeStruct((batch, HIST, GDN_CONV_DIM), F32),
        jax.ShapeDtypeStruct((batch, HIST, SC_DIM), F32),
    )
    return pl.pallas_call(
        functools.partial(_gdn_kernel, tt=tt, chunk=chunk),
        out_shape=out_shape,
        grid=(batch, nt),
        in_specs=[row(GDN_CONV_DIM), row(GDN_QK), row(128), row(3 * SC_DIM),
                  per_b3(GDN_CONV_DIM), per_b3(SC_DIM), state,
                  _const_spec(cw.shape), _const_spec(scw.shape), _const_spec(gp.shape),
                  _const_spec(gng.shape), _const_spec(sng.shape)],
        out_specs=(row(GDN_QK), row(SC_DIM), state, per_b3(GDN_CONV_DIM), per_b3(SC_DIM)),
        scratch_shapes=[pltpu.VMEM((HIST + tt, GDN_CONV_DIM), F32),
                        pltpu.VMEM((HIST + tt, SC_DIM), F32)],
        compiler_params=_params(2),
        name="gdn",
    )(qkv, gz, gab, sbcx, hist_g, hist_s, s0, cw, scw, gp, gng, sng)


def _sb_kernel(q_ref, kd_ref, vd_ref, kp_ref, vp_ref, g_ref, y_ref, acc_ref, c_ref,
               *, tq, tkd, tk, n_past_static):
    lane = lax.broadcasted_iota(jnp.int32, (1, SB_W), 1) // SB_DH
    head_mask = [(lane == h) for h in range(SB_HEADS)]
    q = q_ref[...]
    zero_b = jnp.zeros((), BF16)
    q_heads = [jnp.where(head_mask[h], q, zero_b) for h in range(SB_HEADS)]

    acc_ref[...] = jnp.zeros_like(acc_ref)
    c_ref[...] = jnp.zeros_like(c_ref)

    def block(kb, vb, width, mask):
        r = lax.broadcasted_iota(jnp.int32, (width, width), 0)
        c = lax.broadcasted_iota(jnp.int32, (width, width), 1)
        later = jnp.where(r > c, 1.0, 0.0).astype(BF16)
        pv = jnp.zeros((tq, SB_W), F32)
        for h in range(SB_HEADS):
            z = _dot_nt(q_heads[h], kb)
            sp = _softplus(z)
            l1 = -sp if mask is None else jnp.where(mask, -sp, 0.0)
            l1_hi = l1.astype(BF16)
            l1_lo = (l1 - l1_hi.astype(F32)).astype(BF16)
            rc = _dot(l1_hi, later) + _dot(l1_lo, later)
            carry = c_ref[:, h:h + 1]
            a = jnp.exp((z - sp) + rc + carry)
            if mask is not None:
                a = jnp.where(mask, a, 0.0)
            pv = pv + _dot(a.astype(BF16), jnp.where(head_mask[h], vb, zero_b))
            c_ref[:, h:h + 1] = carry + rc[:, 0:1] + l1[:, 0:1]
        acc_ref[...] += pv

    rq = lax.broadcasted_iota(jnp.int32, (tq, tkd), 0)
    ck = lax.broadcasted_iota(jnp.int32, (tq, tkd), 1)
    block(kd_ref[...], vd_ref[...], tkd, ck < rq)

    if n_past_static is None:
        n_past = pl.program_id(1) * (tq // tk)
    else:
        n_past = n_past_static

    def cond(state):
        j, live = state
        return jnp.logical_and(j >= 0, live)

    def body(state):
        j, _ = state
        start = pl.multiple_of(j * tk, tk)
        kb = kp_ref[pl.ds(start, tk), :].astype(BF16)
        vb = vp_ref[pl.ds(start, tk), :].astype(BF16)
        block(kb, vb, tk, None)
        live = jnp.max(c_ref[:, 0:SB_HEADS]) > SB_LOG_ZERO
        return j - 1, live

    lax.while_loop(cond, body, (n_past - 1, jnp.bool_(True)))

    o = acc_ref[...]
    ms = _dot(o * o, _group_mean_matrix(SB_W, SB_DH), HI)
    y_ref[...] = (o * lax.rsqrt(ms + EPS) * g_ref[...]).astype(BF16)


def _sb_attn(q, kd, vd, kp, vp, g, batch, tq, tkd, tk, n_past_static):
    n = q.shape[0]
    nq = n // batch // tq
    tp = kp.shape[1]
    qrow = pl.BlockSpec((tq, SB_W), lambda b, i: (b * nq + i, 0))
    if kd.ndim == 2:
        drow = pl.BlockSpec((tkd, SB_W), lambda b, i: (b * nq + i, 0))
    else:
        drow = pl.BlockSpec((None, tkd, SB_W), lambda b, i: (b, 0, 0))
    past = pl.BlockSpec((None, tp, SB_W), lambda b, i: (b, 0, 0))
    return pl.pallas_call(
        functools.partial(_sb_kernel, tq=tq, tkd=tkd, tk=tk, n_past_static=n_past_static),
        out_shape=jax.ShapeDtypeStruct((n, SB_W), BF16),
        grid=(batch, nq),
        in_specs=[qrow, drow, drow, past, past, _const_spec(g.shape)],
        out_specs=qrow,
        scratch_shapes=[pltpu.VMEM((tq, SB_W), F32), pltpu.VMEM((tq, 128), F32)],
        compiler_params=_params(2),
        name="sb_attn",
    )(q, kd, vd, kp, vp, g)


def _mixmem_kernel(x_ref, ya_ref, yb_ref, yc_ref, wmix_ref, g_ref, wq_ref, qg_ref,
                   mk_ref, mv_ref, wo_ref, o_ref):
    x1 = (x_ref[...]
          + _dot(ya_ref[...], wmix_ref[0:GDN_QK, :])
          + _dot(yb_ref[...], wmix_ref[GDN_QK:GDN_QK + SC_DIM, :])
          + _dot(yc_ref[...], wmix_ref[GDN_QK + SC_DIM:GDN_QK + SC_DIM + SB_W, :]))
    h = _rms(x1, g_ref[...]).astype(BF16)
    qm = _dot(h, wq_ref[...])
    qg = qg_ref[...]
    outs = []
    for hd in range(MEM_HEADS):
        lanes = slice(MEM_DH * hd, MEM_DH * (hd + 1))
        qh = _rms(qm[:, lanes], qg).astype(BF16)
        kh = mk_ref[:, lanes].astype(BF16)
        vh = mv_ref[:, lanes].astype(BF16)
        s = _dot_nt(qh, kh) * (MEM_DH ** -0.5)
        e = jnp.exp(s - jnp.max(s, axis=-1, keepdims=True))
        p = e / jnp.sum(e, axis=-1, keepdims=True)
        outs.append(_dot(p.astype(BF16), vh).astype(BF16))
    om = jnp.concatenate(outs, axis=1)
    o_ref[...] = x1 + _dot(om, wo_ref[...])


def _mixmem(x, ya, yb, yc, wmix, g, wq, qg, mk, mv, wo, batch, tm):
    n, d = x.shape
    nt = n // batch // tm
    row = lambda width: pl.BlockSpec((tm, width), lambda b, t: (b * nt + t, 0))
    mem = pl.BlockSpec((None,) + mk.shape[1:], lambda b, t: (b, 0, 0))
    return pl.pallas_call(
        _mixmem_kernel,
        out_shape=jax.ShapeDtypeStruct((n, d), F32),
        grid=(batch, nt),
        in_specs=[row(d), row(GDN_QK), row(SC_DIM), row(SB_W),
                  _const_spec(wmix.shape), _const_spec(g.shape), _const_spec(wq.shape),
                  _const_spec(qg.shape), mem, mem, _const_spec(wo.shape)],
        out_specs=row(d),
        compiler_params=_params(2),
        name="mixmem",
    )(x, ya, yb, yc, wmix, g, wq, qg, mk, mv, wo)


def _ffn_kernel(x_ref, g_ref, wg_ref, wu_ref, cw_ref, wd_ref, hist_ref, o_ref, tail_ref, ext,
                *, tm, d_ff):
    t = pl.program_id(1)

    @pl.when(t == 0)
    def _():
        ext[0:HIST, :] = hist_ref[...]

    @pl.when(t > 0)
    def _():
        ext[0:HIST, :] = ext[tm:tm + HIST, :]

    x = x_ref[...]
    h = _rms(x, g_ref[...]).astype(BF16)
    acc = x
    for c0 in range(0, d_ff, FF_CHUNK):
        cols = slice(c0, c0 + FF_CHUNK)
        gate = _dot(h, wg_ref[:, cols])
        ext[HIST:HIST + tm, cols] = gate
        conv = (ext[HIST - 2:HIST - 2 + tm, cols] * cw_ref[0:1, cols]
                + ext[HIST - 1:HIST - 1 + tm, cols] * cw_ref[1:2, cols]
                + gate * cw_ref[2:3, cols])
        up = _dot(h, wu_ref[:, cols])
        hid = (_silu(conv) * up).astype(BF16)
        acc = acc + _dot(hid, wd_ref[cols, :])
    o_ref[...] = acc
    tail_ref[...] = ext[tm:tm + HIST, :]


def _ffn(x, g, wg, wu, cw, wd, hist, batch, tm):
    n, d = x.shape
    d_ff = wg.shape[1]
    nt = n // batch // tm
    row = pl.BlockSpec((tm, d), lambda b, t: (b * nt + t, 0))
    per_b = pl.BlockSpec((None, HIST, d_ff), lambda b, t: (b, 0, 0))
    return pl.pallas_call(
        functools.partial(_ffn_kernel, tm=tm, d_ff=d_ff),
        out_shape=(jax.ShapeDtypeStruct((n, d), F32),
                   jax.ShapeDtypeStruct((batch, HIST, d_ff), F32)),
        grid=(batch, nt),
        in_specs=[row, _const_spec(g.shape), _const_spec(wg.shape), _const_spec(wu.shape),
                  _const_spec(cw.shape), _const_spec(wd.shape), per_b],
        out_specs=(row, per_b),
        scratch_shapes=[pltpu.VMEM((HIST + tm, d_ff), F32)],
        compiler_params=_params(2),
        name="ffn",
    )(x, g, wg, wu, cw, wd, hist)


def _mem_kv_kernel(m_ref, g_ref, wk_ref, wv_ref, kg_ref, k_ref, v_ref):
    m = _rms(m_ref[...], g_ref[...]).astype(BF16)
    k = _dot(m, wk_ref[...])
    kg = kg_ref[...]
    for hd in range(MEM_HEADS):
        lanes = slice(MEM_DH * hd, MEM_DH * (hd + 1))
        k_ref[:, lanes] = _rms(k[:, lanes], kg)
    v_ref[...] = _dot(m, wv_ref[...])


def _mem_kv(mem, g, wk, wv, kg, tm):
    n, d = mem.shape
    w = wk.shape[1]
    row = lambda width: pl.BlockSpec((tm, width), lambda i: (i, 0))
    return pl.pallas_call(
        _mem_kv_kernel,
        out_shape=(jax.ShapeDtypeStruct((n, w), F32), jax.ShapeDtypeStruct((n, w), F32)),
        grid=(n // tm,),
        in_specs=[row(d), _const_spec(g.shape), _const_spec(wk.shape), _const_spec(wv.shape),
                  _const_spec(kg.shape)],
        out_specs=(row(w), row(w)),
        compiler_params=_params(1),
        name="mem_kv",
    )(mem, g, wk, wv, kg)


def _pad_hist(buf):
    return jnp.pad(buf, ((0, 0), (HIST - buf.shape[1], 0), (0, 0)))


def _largest_tile(total, cap):
    t = min(total, cap)
    while total % t:
        t //= 2
    return t


def _layer(x, batch, lw, mem_k, mem_v, s0, hist_g, hist_s, hist_f, sb_past_k, sb_past_v):
    n = x.shape[0]
    t_len = n // batch
    prompt = sb_past_k is None

    qkv, gz, gab, sbcx, cq, ck, cv, ckb, cvb = _in_proj(
        x, lw["norm_mix_g"], lw["w_in"], _largest_tile(n, 512))

    chunk = min(CHUNK, t_len)
    tt = _largest_tile(t_len, 2 * chunk)
    ya, yb, s_new, tail_g, tail_s = _gdn(
        qkv, gz, gab, sbcx, hist_g, hist_s, s0, lw["gdn_conv_w"], lw["sc_conv_w"],
        lw["gate_params"], lw["gdn_norm_g"], lw["sc_norm_g"], batch, tt, chunk)

    if prompt:
        tq = _largest_tile(t_len, 256)
        kp = ckb.reshape(batch, t_len, SB_W)
        vp = cvb.reshape(batch, t_len, SB_W)
        yc = _sb_attn(cq, ckb, cvb, kp, vp, lw["sb_norm_g"], batch, tq, tq, tq, None)
    else:
        tkd = 128
        pad = lambda a: jnp.pad(a.reshape(batch, t_len, SB_W), ((0, 0), (0, tkd - t_len), (0, 0)))
        p_len = sb_past_k.shape[1]
        tk = _largest_tile(p_len, 256)
        yc = _sb_attn(cq, pad(ckb), pad(cvb), sb_past_k, sb_past_v, lw["sb_norm_g"],
                      batch, t_len, tkd, tk, p_len // tk)

    tm = _largest_tile(t_len, 512)
    x = _mixmem(x, ya, yb, yc, lw["w_mix_out"], lw["norm_mem_g"], lw["w_mq"], lw["mq_norm_g"],
                mem_k, mem_v, lw["w_mo"], batch, tm)
    x, tail_f = _ffn(x, lw["norm_ffn_g"], lw["w_gate"], lw["w_up"], lw["ffn_conv_w"],
                     lw["w_down"], hist_f, batch, tm)
    return x, s_new, tail_g[:, HIST - 3:], tail_s[:, HIST - 2:], ck, cv, tail_f[:, HIST - 2:]


def kernel(x_prompt, x_sample, mem_prompt, state_gdn, cache_gdn_conv, cache_sc_conv, cache_sb_k, cache_sb_v, cache_mem_k, cache_mem_v, cache_ffn_conv, norm_mix_g, w_in, gdn_conv_w, gdn_A_log, gdn_dt_bias, gdn_norm_g, sc_conv_w, sc_norm_g, sb_norm_g, w_mix_out, norm_mem_g, mem_in_norm_g, w_mq, w_mk, w_mv, mq_norm_g, mk_norm_g, w_mo, norm_ffn_g, w_gate, w_up, ffn_conv_w, w_down):
    depth = w_in.shape[0]
    bp, tp, d = x_prompt.shape
    bs, ts, _ = x_sample.shape
    n_mem = mem_prompt.shape[1]
    d_ff = w_gate.shape[2]
    n_gate = 2 * GDN_QK + 2 * GDN_QK

    xp = x_prompt.reshape(bp * tp, d)
    xs = x_sample.reshape(bs * ts, d)
    mem = mem_prompt.reshape(bp * n_mem, d)
    zeros = lambda b, c: jnp.zeros((b, HIST, c), F32)

    outs = [[] for _ in range(14)]
    for l in range(depth):
        w = w_in[l]
        w_packed = jnp.concatenate(
            [w[:, :n_gate + 2 * GDN_HEADS],
             jnp.zeros((d, 128 - 2 * GDN_HEADS), w.dtype),
             w[:, n_gate + 2 * GDN_HEADS:]], axis=1).astype(BF16)
        gate_params = jnp.zeros((8, 128), F32)
        gate_params = gate_params.at[0, GDN_HEADS:2 * GDN_HEADS].set(gdn_A_log[l])
        gate_params = gate_params.at[1, GDN_HEADS:2 * GDN_HEADS].set(gdn_dt_bias[l])
        row = lambda a: a[l].reshape(1, -1)
        lw = dict(
            norm_mix_g=row(norm_mix_g), w_in=w_packed, gdn_conv_w=gdn_conv_w[l],
            gate_params=gate_params, gdn_norm_g=row(gdn_norm_g), sc_conv_w=sc_conv_w[l],
            sc_norm_g=row(sc_norm_g), sb_norm_g=row(sb_norm_g),
            w_mix_out=w_mix_out[l].astype(BF16), norm_mem_g=row(norm_mem_g),
            w_mq=w_mq[l].astype(BF16), mq_norm_g=row(mq_norm_g), w_mo=w_mo[l].astype(BF16),
            norm_ffn_g=row(norm_ffn_g), w_gate=w_gate[l].astype(BF16), w_up=w_up[l].astype(BF16),
            ffn_conv_w=ffn_conv_w[l], w_down=w_down[l].astype(BF16))

        mk, mv = _mem_kv(mem, row(mem_in_norm_g), w_mk[l].astype(BF16), w_mv[l].astype(BF16),
                         row(mk_norm_g), _largest_tile(bp * n_mem, 512))
        mk3 = mk.reshape(bp, n_mem, MEM_HEADS * MEM_DH)
        mv3 = mv.reshape(bp, n_mem, MEM_HEADS * MEM_DH)
        xp, s_n, gc_n, sc_n, k_n, v_n, fc_n = _layer(
            xp, bp, lw, mk3, mv3, jnp.zeros((bp, GDN_HEADS, GDN_DK, GDN_DK), F32),
            zeros(bp, GDN_CONV_DIM), zeros(bp, SC_DIM), zeros(bp, d_ff), None, None)
        for i, a in enumerate((s_n, gc_n, sc_n, k_n.reshape(bp, tp, SB_HEADS, SB_DH),
                               v_n.reshape(bp, tp, SB_HEADS, SB_DH),
                               mk.reshape(bp, n_mem, MEM_HEADS, MEM_DH),
                               mv.reshape(bp, n_mem, MEM_HEADS, MEM_DH), fc_n)):
            outs[i].append(a)

        p_len = cache_sb_k.shape[2]
        xs, s_n, gc_n, sc_n, k_n, v_n, fc_n = _layer(
            xs, bs, lw,
            cache_mem_k[l].reshape(bs, n_mem, MEM_HEADS * MEM_DH),
            cache_mem_v[l].reshape(bs, n_mem, MEM_HEADS * MEM_DH),
            state_gdn[l], _pad_hist(cache_gdn_conv[l]), _pad_hist(cache_sc_conv[l]),
            _pad_hist(cache_ffn_conv[l]),
            cache_sb_k[l].reshape(bs, p_len, SB_W), cache_sb_v[l].reshape(bs, p_len, SB_W))
        for i, a in enumerate((s_n, gc_n, sc_n, k_n.reshape(bs, ts, SB_HEADS, SB_DH),
                               v_n.reshape(bs, ts, SB_HEADS, SB_DH), fc_n)):
            outs[8 + i].append(a)

    return (xp.reshape(bp, tp, d), xs.reshape(bs, ts, d)) + tuple(jnp.stack(o) for o in outs)
```

```python
import functools

import jax
import jax.numpy as jnp
from jax import lax
from jax.experimental import pallas as pl
from jax.experimental.pallas import tpu as pltpu

F32 = jnp.float32
BF16 = jnp.bfloat16
HI = lax.Precision.HIGHEST
EPS = 1e-6

GDN_HEADS = 4
GDN_DK = 128
GDN_QK = 512
GDN_CONV_DIM = 1536
SC_DIM = 256
SB_HEADS = 4
SB_DH = 64
SB_W = 256
MEM_HEADS = 4
MEM_DH = 128
CHUNK = 64
HIST = 8
FF_CHUNK = 256
SB_LOG_ZERO = -104.0
VMEM_LIMIT = 56 * 1024 * 1024

C_QKV, C_GZ, C_GAB, C_SBCX, C_CQ, C_CK, C_CV, C_END = 0, 1536, 2048, 2176, 2944, 3200, 3456, 3712


def _dot(a, b, precision=None):
    return jnp.dot(a, b, preferred_element_type=F32, precision=precision)


def _dot_nt(a, b, precision=None):
    return lax.dot_general(a, b, (((1,), (1,)), ((), ())),
                           preferred_element_type=F32, precision=precision)


def _dot_tn(a, b, precision=None):
    return lax.dot_general(a, b, (((0,), (0,)), ((), ())),
                           preferred_element_type=F32, precision=precision)


def _rms(x, g):
    return x * lax.rsqrt(jnp.mean(x * x, axis=-1, keepdims=True) + EPS) * g


def _sigmoid(x):
    return 1.0 / (1.0 + jnp.exp(-x))


def _silu(x):
    return x * _sigmoid(x)


def _softplus(x):
    return jnp.maximum(x, 0.0) + jnp.log1p(jnp.exp(-jnp.abs(x)))


def _group_mean_matrix(width, group):
    r = lax.broadcasted_iota(jnp.int32, (width, width), 0) // group
    c = lax.broadcasted_iota(jnp.int32, (width, width), 1) // group
    return jnp.where(r == c, 1.0 / group, 0.0).astype(F32)


def _params(n_axes):
    return pltpu.CompilerParams(dimension_semantics=("arbitrary",) * n_axes,
                                vmem_limit_bytes=VMEM_LIMIT)


def _const_spec(shape):
    nd = len(shape)
    return pl.BlockSpec(shape, lambda *_: (0,) * nd, pipeline_mode=pl.Buffered(1))


def _in_proj_kernel(x_ref, g_ref, w_ref, qkv_ref, gz_ref, gab_ref, sbcx_ref,
                    cq_ref, ck_ref, cv_ref, ckb_ref, cvb_ref):
    h = _rms(x_ref[...], g_ref[...]).astype(BF16)

    def proj(c0, c1):
        return _dot(h, w_ref[:, c0:c1])

    for c in range(0, GDN_CONV_DIM, 512):
        qkv_ref[:, c:c + 512] = proj(C_QKV + c, C_QKV + c + 512)
    gz_ref[...] = proj(C_GZ, C_GAB)
    gab_ref[...] = proj(C_GAB, C_SBCX)
    for c in range(0, 3 * SC_DIM, 256):
        sbcx_ref[:, c:c + 256] = proj(C_SBCX + c, C_SBCX + c + 256)
    cq_ref[...] = (proj(C_CQ, C_CK) * (SB_DH ** -0.5)).astype(BF16)
    k = proj(C_CK, C_CV)
    ck_ref[...] = k
    ckb_ref[...] = k.astype(BF16)
    v = proj(C_CV, C_END)
    cv_ref[...] = v
    cvb_ref[...] = v.astype(BF16)


def _in_proj(x, g, w, tm):
    n, d = x.shape
    row = lambda width: pl.BlockSpec((tm, width), lambda i: (i, 0))
    out_shape = (
        jax.ShapeDtypeStruct((n, GDN_CONV_DIM), F32),
        jax.ShapeDtypeStruct((n, GDN_QK), F32),
        jax.ShapeDtypeStruct((n, 128), F32),
        jax.ShapeDtypeStruct((n, 3 * SC_DIM), F32),
        jax.ShapeDtypeStruct((n, SB_W), BF16),
        jax.ShapeDtypeStruct((n, SB_W), F32),
        jax.ShapeDtypeStruct((n, SB_W), F32),
        jax.ShapeDtypeStruct((n, SB_W), BF16),
        jax.ShapeDtypeStruct((n, SB_W), BF16),
    )
    return pl.pallas_call(
        _in_proj_kernel,
        out_shape=out_shape,
        grid=(n // tm,),
        in_specs=[row(d), _const_spec((1, d)), _const_spec(w.shape)],
        out_specs=(row(GDN_CONV_DIM), row(GDN_QK), row(128), row(3 * SC_DIM),
                   row(SB_W), row(SB_W), row(SB_W), row(SB_W), row(SB_W)),
        compiler_params=_params(1),
        name="in_proj",
    )(x, g, w)


def _split2(x):
    hi = x.astype(BF16)
    return hi, (x - hi.astype(F32)).astype(BF16)


def _dot2(a, b_exact):
    hi, lo = _split2(a)
    return _dot(hi, b_exact) + _dot(lo, b_exact)


def _gdn_kernel(qkv_ref, gz_ref, gab_ref, sbcx_ref, hg_ref, hs_ref, s0_ref,
                cw_ref, scw_ref, gp_ref, gng_ref, sng_ref,
                ya_ref, yb_ref, s_ref, tg_ref, ts_ref, extg, exts, *, tt, chunk):
    t = pl.program_id(1)
    nh = GDN_HEADS
    sw = nh * chunk
    sec = max(sw, 128)
    n_chunks = tt // chunk

    @pl.when(t == 0)
    def _():
        extg[0:HIST, :] = hg_ref[...]
        exts[0:HIST, :] = hs_ref[...]
        s_ref[...] = s0_ref[...]

    @pl.when(t > 0)
    def _():
        extg[0:HIST, :] = extg[tt:tt + HIST, :]
        exts[0:HIST, :] = exts[tt:tt + HIST, :]

    x = qkv_ref[...]
    extg[HIST:HIST + tt, :] = x
    tg_ref[...] = x[tt - HIST:tt, :]
    conv = (extg[HIST - 3:HIST - 3 + tt, :] * cw_ref[0:1, :]
            + extg[HIST - 2:HIST - 2 + tt, :] * cw_ref[1:2, :]
            + extg[HIST - 1:HIST - 1 + tt, :] * cw_ref[2:3, :]
            + x * cw_ref[3:4, :])
    act = _silu(conv)

    def l2n(a):
        parts = []
        for h in range(nh):
            ah = a[:, GDN_DK * h:GDN_DK * (h + 1)]
            parts.append(ah * lax.rsqrt(jnp.sum(ah * ah, axis=-1, keepdims=True) + EPS))
        return jnp.concatenate(parts, axis=1)

    qn = l2n(act[:, 0:GDN_QK]) * (GDN_DK ** -0.5)
    kn = l2n(act[:, GDN_QK:2 * GDN_QK])
    vv = act[:, 2 * GDN_QK:3 * GDN_QK]

    slab = gab_ref[...]
    beta_s = _sigmoid(slab)
    g_s = -jnp.exp(gp_ref[0:1, :]) * _softplus(slab + gp_ref[1:2, :])
    ri = lax.broadcasted_iota(jnp.int32, (tt, tt), 0)
    ci = lax.broadcasted_iota(jnp.int32, (tt, tt), 1)
    tri = jnp.where(jnp.logical_and(ri // chunk == ci // chunk, ri >= ci), 1.0, 0.0).astype(BF16)
    g_hi, g_lo = _split2(g_s)
    gc = _dot(tri, g_hi) + _dot(tri, g_lo)
    lane128 = lax.broadcasted_iota(jnp.int32, (1, 128), 1)
    slab2 = jnp.where(lane128 < nh, beta_s, gc)
    ew = 2 * GDN_QK + 2 * sec
    er = lax.broadcasted_iota(jnp.int32, (128, ew), 0)
    ec = lax.broadcasted_iota(jnp.int32, (128, ew), 1)
    src_row = jnp.where(ec < GDN_QK, ec // GDN_DK,
              jnp.where(ec < 2 * GDN_QK, nh + (ec - GDN_QK) // GDN_DK,
              jnp.where(ec < 2 * GDN_QK + sec, (ec - 2 * GDN_QK) // chunk,
                        nh + (ec - 2 * GDN_QK - sec) // chunk)))
    expand = jnp.where(er == src_row, 1.0, 0.0).astype(BF16)
    xp = _dot2(slab2, expand)
    b512 = xp[:, 0:GDN_QK]
    gc512 = xp[:, GDN_QK:2 * GDN_QK]
    b_s = xp[:, 2 * GDN_QK:2 * GDN_QK + sec][:, 0:sw]
    gc_s = xp[:, 2 * GDN_QK + sec:2 * GDN_QK + 2 * sec][:, 0:sw]
    eg512 = jnp.exp(gc512)

    qb = qn.astype(BF16)
    kb = kn.astype(BF16)
    qe = (qn * eg512).astype(BF16)
    rv = vv * b512
    rk = kn * (b512 * eg512)

    r_s = lax.broadcasted_iota(jnp.int32, (chunk, sw), 0)
    c_s = lax.broadcasted_iota(jnp.int32, (chunk, sw), 1)
    c_in = c_s - (c_s // chunk) * chunk
    causal_s = r_s >= c_in
    strict_s = r_s > c_in
    eye_s = jnp.where(r_s == c_in, 1.0, 0.0).astype(F32)
    blk_s = lax.broadcasted_iota(jnp.int32, (1, sw), 1) // chunk
    bd_r = lax.broadcasted_iota(jnp.int32, (sw, sw), 0) // chunk
    bd_c = lax.broadcasted_iota(jnp.int32, (sw, sw), 1) // chunk
    bd_mask = bd_r == bd_c
    bdw_r = lax.broadcasted_iota(jnp.int32, (sw, GDN_QK), 0) // chunk
    bdw_c = lax.broadcasted_iota(jnp.int32, (sw, GDN_QK), 1) // GDN_DK
    bdw_mask = bdw_r == bdw_c
    zero_b = jnp.zeros((), BF16)
    ones_cc = jnp.ones((chunk, chunk), BF16)

    def stack_heads(a):
        return jnp.concatenate([a[:, GDN_DK * h:GDN_DK * (h + 1)] for h in range(nh)], axis=0)

    def diag_blocks(g):
        out = g[0:chunk, :]
        for h in range(1, nh):
            out = jnp.where(blk_s == h, g[h * chunk:(h + 1) * chunk, :], out)
        return out

    def bd(a):
        return jnp.where(bd_mask, jnp.concatenate([a] * nh, axis=0), zero_b)

    def bdw(a):
        return jnp.where(bdw_mask, jnp.concatenate([a] * nh, axis=0), zero_b)

    pw, inv, qkd = [], [], []
    for c in range(n_chunks):
        rows = slice(c * chunk, (c + 1) * chunk)
        kst = stack_heads(kb[rows])
        kk = diag_blocks(_dot_nt(kst, kst))
        qk = diag_blocks(_dot_nt(stack_heads(qb[rows]), kst))
        gcs = gc_s[rows]
        g_hi, g_lo = _split2(eye_s * gcs)
        row = _dot(ones_cc, g_hi) + _dot(ones_cc, g_lo)
        decay = jnp.exp(jnp.where(causal_s, gcs - row, -jnp.inf))
        nm = jnp.where(strict_s, -(b_s[rows] * kk * decay), 0.0)
        pw.append(_split2(nm))
        inv.append(eye_s + nm)
        qkd.append((qk * decay).astype(BF16))

    n_levels = chunk.bit_length() - 2
    for _ in range(n_levels):
        for c in range(n_chunks):
            p_hi, p_lo = pw[c]
            b_hi, b_lo = bd(p_hi), bd(p_lo)
            sq = _dot(p_hi, b_hi) + (_dot(p_hi, b_lo) + _dot(p_lo, b_hi))
            s_hi, s_lo = _split2(sq)
            pw[c] = (s_hi, s_lo)
            b_hi, b_lo = bd(s_hi), bd(s_lo)
            i_hi, i_lo = _split2(inv[c])
            inv[c] = inv[c] + (_dot(i_hi, b_hi) + (_dot(i_hi, b_lo) + _dot(i_lo, b_hi)))

    u_all, w_all = [], []
    for c in range(n_chunks):
        rows = slice(c * chunk, (c + 1) * chunk)
        i_hi, i_lo = _split2(inv[c])
        v_hi, v_lo = _split2(rv[rows])
        k_hi, k_lo = _split2(rk[rows])
        bv_hi, bk_hi = bdw(v_hi), bdw(k_hi)
        u_all.append(_dot(i_hi, bv_hi) + (_dot(i_hi, bdw(v_lo)) + _dot(i_lo, bv_hi)))
        w_all.append((_dot(i_hi, bk_hi) + (_dot(i_hi, bdw(k_lo)) + _dot(i_lo, bk_hi))).astype(BF16))

    gng = gng_ref[...]
    for c in range(n_chunks):
        rows = slice(c * chunk, (c + 1) * chunk)
        g_last = gc512[(c + 1) * chunk - 1:(c + 1) * chunk, :]
        kd = (kn[rows] * jnp.exp(g_last - gc512[rows])).astype(BF16)
        eg_last = jnp.exp(g_last)
        s_old = [s_ref[h] for h in range(nh)]
        s_b = [s.astype(BF16) for s in s_old]
        v_new = jnp.concatenate(
            [u_all[c][:, GDN_DK * h:GDN_DK * (h + 1)]
             - _dot(w_all[c][:, GDN_DK * h:GDN_DK * (h + 1)], s_b[h]) for h in range(nh)], axis=1)
        vb = v_new.astype(BF16)
        o_state = jnp.concatenate(
            [_dot(qe[rows][:, GDN_DK * h:GDN_DK * (h + 1)], s_b[h]) for h in range(nh)], axis=1)
        o = o_state + _dot(qkd[c], bdw(vb))
        for h in range(nh):
            lanes = slice(GDN_DK * h, GDN_DK * (h + 1))
            s_ref[h] = s_old[h] * eg_last[:, lanes] + _dot_tn(kd[:, lanes], vb[:, lanes])
        z = gz_ref[c * chunk:(c + 1) * chunk, :]
        ya_ref[c * chunk:(c + 1) * chunk, :] = (jnp.concatenate(
            [_rms(o[:, GDN_DK * h:GDN_DK * (h + 1)], gng) for h in range(nh)], axis=1)
            * _silu(z)).astype(BF16)

    sbcx = sbcx_ref[...]
    s_b = sbcx[:, 0:SC_DIM]
    pre = sbcx[:, SC_DIM:2 * SC_DIM] * sbcx[:, 2 * SC_DIM:3 * SC_DIM]
    exts[HIST:HIST + tt, :] = pre
    ts_ref[...] = pre[tt - HIST:tt, :]
    u_c = (exts[HIST - 2:HIST - 2 + tt, :] * scw_ref[0:1, :]
           + exts[HIST - 1:HIST - 1 + tt, :] * scw_ref[1:2, :]
           + pre * scw_ref[2:3, :])
    yb = s_b * u_c
    ms = _dot(yb * yb, _group_mean_matrix(SC_DIM, SC_DIM // 4), HI)
    yb_ref[...] = (yb * lax.rsqrt(ms + EPS) * sng_ref[...]).astype(BF16)


def _gdn(qkv, gz, gab, sbcx, hist_g, hist_s, s0, cw, scw, gp, gng, sng, batch, tt, chunk):
    n = qkv.shape[0]
    nt = n // batch // tt
    row = lambda width: pl.BlockSpec((tt, width), lambda b, t: (b * nt + t, 0))
    per_b3 = lambda width: pl.BlockSpec((None, HIST, width), lambda b, t: (b, 0, 0))
    state = pl.BlockSpec((None, GDN_HEADS, GDN_DK, GDN_DK), lambda b, t: (b, 0, 0, 0))
    out_shape = (
        jax.ShapeDtypeStruct((n, GDN_QK), BF16),
        jax.ShapeDtypeStruct((n, SC_DIM), BF16),
        jax.ShapeDtypeStruct((batch, GDN_HEADS, GDN_DK, GDN_DK), F32),
        jax.ShapeDtypeStruct((batch, HIST, GDN_CONV_DIM), F32),
        jax.ShapeDtypeStruct((batch, HIST, SC_DIM), F32),
    )
    return pl.pallas_call(
        functools.partial(_gdn_kernel, tt=tt, chunk=chunk),
        out_shape=out_shape,
        grid=(batch, nt),
        in_specs=[row(GDN_CONV_DIM), row(GDN_QK), row(128), row(3 * SC_DIM),
                  per_b3(GDN_CONV_DIM), per_b3(SC_DIM), state,
                  _const_spec(cw.shape), _const_spec(scw.shape), _const_spec(gp.shape),
                  _const_spec(gng.shape), _const_spec(sng.shape)],
        out_specs=(row(GDN_QK), row(SC_DIM), state, per_b3(GDN_CONV_DIM), per_b3(SC_DIM)),
        scratch_shapes=[pltpu.VMEM((HIST + tt, GDN_CONV_DIM), F32),
                        pltpu.VMEM((HIST + tt, SC_DIM), F32)],
        compiler_params=_params(2),
        name="gdn",
    )(qkv, gz, gab, sbcx, hist_g, hist_s, s0, cw, scw, gp, gng, sng)


def _sb_kernel(q_ref, kd_ref, vd_ref, kp_ref, vp_ref, g_ref, y_ref, acc_ref, c_ref,
               *, tq, tkd, tk, n_past_static):
    lane = lax.broadcasted_iota(jnp.int32, (1, SB_W), 1) // SB_DH
    head_mask = [(lane == h) for h in range(SB_HEADS)]
    q = q_ref[...]
    zero_b = jnp.zeros((), BF16)
    q_heads = [jnp.where(head_mask[h], q, zero_b) for h in range(SB_HEADS)]

    acc_ref[...] = jnp.zeros_like(acc_ref)
    c_ref[...] = jnp.zeros_like(c_ref)

    def block(kb, vb, width, mask):
        r = lax.broadcasted_iota(jnp.int32, (width, width), 0)
        c = lax.broadcasted_iota(jnp.int32, (width, width), 1)
        later = jnp.where(r > c, 1.0, 0.0).astype(BF16)
        pv = jnp.zeros((tq, SB_W), F32)
        for h in range(SB_HEADS):
            z = _dot_nt(q_heads[h], kb)
            sp = _softplus(z)
            l1 = -sp if mask is None else jnp.where(mask, -sp, 0.0)
            l1_hi = l1.astype(BF16)
            l1_lo = (l1 - l1_hi.astype(F32)).astype(BF16)
            rc = _dot(l1_hi, later) + _dot(l1_lo, later)
            carry = c_ref[:, h:h + 1]
            a = jnp.exp((z - sp) + rc + carry)
            if mask is not None:
                a = jnp.where(mask, a, 0.0)
            pv = pv + _dot(a.astype(BF16), jnp.where(head_mask[h], vb, zero_b))
            c_ref[:, h:h + 1] = carry + rc[:, 0:1] + l1[:, 0:1]
        acc_ref[...] += pv

    rq = lax.broadcasted_iota(jnp.int32, (tq, tkd), 0)
    ck = lax.broadcasted_iota(jnp.int32, (tq, tkd), 1)
    block(kd_ref[...], vd_ref[...], tkd, ck < rq)

    if n_past_static is None:
        n_past = pl.program_id(1) * (tq // tk)
    else:
        n_past = n_past_static

    def cond(state):
        j, live = state
        return jnp.logical_and(j >= 0, live)

    def body(state):
        j, _ = state
        start = pl.multiple_of(j * tk, tk)
        kb = kp_ref[pl.ds(start, tk), :].astype(BF16)
        vb = vp_ref[pl.ds(start, tk), :].astype(BF16)
        block(kb, vb, tk, None)
        live = jnp.max(c_ref[:, 0:SB_HEADS]) > SB_LOG_ZERO
        return j - 1, live

    lax.while_loop(cond, body, (n_past - 1, jnp.bool_(True)))

    o = acc_ref[...]
    ms = _dot(o * o, _group_mean_matrix(SB_W, SB_DH), HI)
    y_ref[...] = (o * lax.rsqrt(ms + EPS) * g_ref[...]).astype(BF16)


def _sb_attn(q, kd, vd, kp, vp, g, batch, tq, tkd, tk, n_past_static):
    n = q.shape[0]
    nq = n // batch // tq
    tp = kp.shape[1]
    qrow = pl.BlockSpec((tq, SB_W), lambda b, i: (b * nq + i, 0))
    if kd.ndim == 2:
        drow = pl.BlockSpec((tkd, SB_W), lambda b, i: (b * nq + i, 0))
    else:
        drow = pl.BlockSpec((None, tkd, SB_W), lambda b, i: (b, 0, 0))
    past = pl.BlockSpec((None, tp, SB_W), lambda b, i: (b, 0, 0))
    return pl.pallas_call(
        functools.partial(_sb_kernel, tq=tq, tkd=tkd, tk=tk, n_past_static=n_past_static),
        out_shape=jax.ShapeDtypeStruct((n, SB_W), BF16),
        grid=(batch, nq),
        in_specs=[qrow, drow, drow, past, past, _const_spec(g.shape)],
        out_specs=qrow,
        scratch_shapes=[pltpu.VMEM((tq, SB_W), F32), pltpu.VMEM((tq, 128), F32)],
        compiler_params=_params(2),
        name="sb_attn",
    )(q, kd, vd, kp, vp, g)


def _mixmem_kernel(x_ref, ya_ref, yb_ref, yc_ref, wmix_ref, g_ref, wq_ref, qg_ref,
                   mk_ref, mv_ref, wo_ref, o_ref):
    x1 = (x_ref[...]
          + _dot(ya_ref[...], wmix_ref[0:GDN_QK, :])
          + _dot(yb_ref[...], wmix_ref[GDN_QK:GDN_QK + SC_DIM, :])
          + _dot(yc_ref[...], wmix_ref[GDN_QK + SC_DIM:GDN_QK + SC_DIM + SB_W, :]))
    h = _rms(x1, g_ref[...]).astype(BF16)
    qm = _dot(h, wq_ref[...])
    qg = qg_ref[...]
    outs = []
    for hd in range(MEM_HEADS):
        lanes = slice(MEM_DH * hd, MEM_DH * (hd + 1))
        qh = _rms(qm[:, lanes], qg).astype(BF16)
        kh = mk_ref[:, lanes].astype(BF16)
        vh = mv_ref[:, lanes].astype(BF16)
        s = _dot_nt(qh, kh) * (MEM_DH ** -0.5)
        e = jnp.exp(s - jnp.max(s, axis=-1, keepdims=True))
        p = e / jnp.sum(e, axis=-1, keepdims=True)
        outs.append(_dot(p.astype(BF16), vh).astype(BF16))
    om = jnp.concatenate(outs, axis=1)
    o_ref[...] = x1 + _dot(om, wo_ref[...])


def _mixmem(x, ya, yb, yc, wmix, g, wq, qg, mk, mv, wo, batch, tm):
    n, d = x.shape
    nt = n // batch // tm
    row = lambda width: pl.BlockSpec((tm, width), lambda b, t: (b * nt + t, 0))
    mem = pl.BlockSpec((None,) + mk.shape[1:], lambda b, t: (b, 0, 0))
    return pl.pallas_call(
        _mixmem_kernel,
        out_shape=jax.ShapeDtypeStruct((n, d), F32),
        grid=(batch, nt),
        in_specs=[row(d), row(GDN_QK), row(SC_DIM), row(SB_W),
                  _const_spec(wmix.shape), _const_spec(g.shape), _const_spec(wq.shape),
                  _const_spec(qg.shape), mem, mem, _const_spec(wo.shape)],
        out_specs=row(d),
        compiler_params=_params(2),
        name="mixmem",
    )(x, ya, yb, yc, wmix, g, wq, qg, mk, mv, wo)


def _ffn_kernel(x_ref, g_ref, wg_ref, wu_ref, cw_ref, wd_ref, hist_ref, o_ref, tail_ref, ext,
                *, tm, d_ff):
    t = pl.program_id(1)

    @pl.when(t == 0)
    def _():
        ext[0:HIST, :] = hist_ref[...]

    @pl.when(t > 0)
    def _():
        ext[0:HIST, :] = ext[tm:tm + HIST, :]

    x = x_ref[...]
    h = _rms(x, g_ref[...]).astype(BF16)
    acc = x
    for c0 in range(0, d_ff, FF_CHUNK):
        cols = slice(c0, c0 + FF_CHUNK)
        gate = _dot(h, wg_ref[:, cols])
        ext[HIST:HIST + tm, cols] = gate
        conv = (ext[HIST - 2:HIST - 2 + tm, cols] * cw_ref[0:1, cols]
                + ext[HIST - 1:HIST - 1 + tm, cols] * cw_ref[1:2, cols]
                + gate * cw_ref[2:3, cols])
        up = _dot(h, wu_ref[:, cols])
        hid = (_silu(conv) * up).astype(BF16)
        acc = acc + _dot(hid, wd_ref[cols, :])
    o_ref[...] = acc
    tail_ref[...] = ext[tm:tm + HIST, :]


def _ffn(x, g, wg, wu, cw, wd, hist, batch, tm):
    n, d = x.shape
    d_ff = wg.shape[1]
    nt = n // batch // tm
    row = pl.BlockSpec((tm, d), lambda b, t: (b * nt + t, 0))
    per_b = pl.BlockSpec((None, HIST, d_ff), lambda b, t: (b, 0, 0))
    return pl.pallas_call(
        functools.partial(_ffn_kernel, tm=tm, d_ff=d_ff),
        out_shape=(jax.ShapeDtypeStruct((n, d), F32),
                   jax.ShapeDtypeStruct((batch, HIST, d_ff), F32)),
        grid=(batch, nt),
        in_specs=[row, _const_spec(g.shape), _const_spec(wg.shape), _const_spec(wu.shape),
                  _const_spec(cw.shape), _const_spec(wd.shape), per_b],
        out_specs=(row, per_b),
        scratch_shapes=[pltpu.VMEM((HIST + tm, d_ff), F32)],
        compiler_params=_params(2),
        name="ffn",
    )(x, g, wg, wu, cw, wd, hist)


def _mem_kv_kernel(m_ref, g_ref, wk_ref, wv_ref, kg_ref, k_ref, v_ref):
    m = _rms(m_ref[...], g_ref[...]).astype(BF16)
    k = _dot(m, wk_ref[...])
    kg = kg_ref[...]
    for hd in range(MEM_HEADS):
        lanes = slice(MEM_DH * hd, MEM_DH * (hd + 1))
        k_ref[:, lanes] = _rms(k[:, lanes], kg)
    v_ref[...] = _dot(m, wv_ref[...])


def _mem_kv(mem, g, wk, wv, kg, tm):
    n, d = mem.shape
    w = wk.shape[1]
    row = lambda width: pl.BlockSpec((tm, width), lambda i: (i, 0))
    return pl.pallas_call(
        _mem_kv_kernel,
        out_shape=(jax.ShapeDtypeStruct((n, w), F32), jax.ShapeDtypeStruct((n, w), F32)),
        grid=(n // tm,),
        in_specs=[row(d), _const_spec(g.shape), _const_spec(wk.shape), _const_spec(wv.shape),
                  _const_spec(kg.shape)],
        out_specs=(row(w), row(w)),
        compiler_params=_params(1),
        name="mem_kv",
    )(mem, g, wk, wv, kg)


def _pad_hist(buf):
    return jnp.pad(buf, ((0, 0), (HIST - buf.shape[1], 0), (0, 0)))


def _largest_tile(total, cap):
    t = min(total, cap)
    while total % t:
        t //= 2
    return t


def _layer(x, batch, lw, mem_k, mem_v, s0, hist_g, hist_s, hist_f, sb_past_k, sb_past_v):
    n = x.shape[0]
    t_len = n // batch
    prompt = sb_past_k is None

    qkv, gz, gab, sbcx, cq, ck, cv, ckb, cvb = _in_proj(
        x, lw["norm_mix_g"], lw["w_in"], _largest_tile(n, 512))

    chunk = min(CHUNK, t_len)
    tt = _largest_tile(t_len, 4 * chunk)
    ya, yb, s_new, tail_g, tail_s = _gdn(
        qkv, gz, gab, sbcx, hist_g, hist_s, s0, lw["gdn_conv_w"], lw["sc_conv_w"],
        lw["gate_params"], lw["gdn_norm_g"], lw["sc_norm_g"], batch, tt, chunk)

    if prompt:
        tq = _largest_tile(t_len, 256)
        kp = ckb.reshape(batch, t_len, SB_W)
        vp = cvb.reshape(batch, t_len, SB_W)
        yc = _sb_attn(cq, ckb, cvb, kp, vp, lw["sb_norm_g"], batch, tq, tq, tq, None)
    else:
        tkd = 128
        pad = lambda a: jnp.pad(a.reshape(batch, t_len, SB_W), ((0, 0), (0, tkd - t_len), (0, 0)))
        p_len = sb_past_k.shape[1]
        tk = _largest_tile(p_len, 256)
        yc = _sb_attn(cq, pad(ckb), pad(cvb), sb_past_k, sb_past_v, lw["sb_norm_g"],
                      batch, t_len, tkd, tk, p_len // tk)

    tm = _largest_tile(t_len, 512)
    x = _mixmem(x, ya, yb, yc, lw["w_mix_out"], lw["norm_mem_g"], lw["w_mq"], lw["mq_norm_g"],
                mem_k, mem_v, lw["w_mo"], batch, tm)
    x, tail_f = _ffn(x, lw["norm_ffn_g"], lw["w_gate"], lw["w_up"], lw["ffn_conv_w"],
                     lw["w_down"], hist_f, batch, tm)
    return x, s_new, tail_g[:, HIST - 3:], tail_s[:, HIST - 2:], ck, cv, tail_f[:, HIST - 2:]


def kernel(x_prompt, x_sample, mem_prompt, state_gdn, cache_gdn_conv, cache_sc_conv, cache_sb_k, cache_sb_v, cache_mem_k, cache_mem_v, cache_ffn_conv, norm_mix_g, w_in, gdn_conv_w, gdn_A_log, gdn_dt_bias, gdn_norm_g, sc_conv_w, sc_norm_g, sb_norm_g, w_mix_out, norm_mem_g, mem_in_norm_g, w_mq, w_mk, w_mv, mq_norm_g, mk_norm_g, w_mo, norm_ffn_g, w_gate, w_up, ffn_conv_w, w_down):
    depth = w_in.shape[0]
    bp, tp, d = x_prompt.shape
    bs, ts, _ = x_sample.shape
    n_mem = mem_prompt.shape[1]
    d_ff = w_gate.shape[2]
    n_gate = 2 * GDN_QK + 2 * GDN_QK

    xp = x_prompt.reshape(bp * tp, d)
    xs = x_sample.reshape(bs * ts, d)
    mem = mem_prompt.reshape(bp * n_mem, d)
    zeros = lambda b, c: jnp.zeros((b, HIST, c), F32)

    outs = [[] for _ in range(14)]
    for l in range(depth):
        w = w_in[l]
        w_packed = jnp.concatenate(
            [w[:, :n_gate + 2 * GDN_HEADS],
             jnp.zeros((d, 128 - 2 * GDN_HEADS), w.dtype),
             w[:, n_gate + 2 * GDN_HEADS:]], axis=1).astype(BF16)
        gate_params = jnp.zeros((8, 128), F32)
        gate_params = gate_params.at[0, GDN_HEADS:2 * GDN_HEADS].set(gdn_A_log[l])
        gate_params = gate_params.at[1, GDN_HEADS:2 * GDN_HEADS].set(gdn_dt_bias[l])
        row = lambda a: a[l].reshape(1, -1)
        lw = dict(
            norm_mix_g=row(norm_mix_g), w_in=w_packed, gdn_conv_w=gdn_conv_w[l],
            gate_params=gate_params, gdn_norm_g=row(gdn_norm_g), sc_conv_w=sc_conv_w[l],
            sc_norm_g=row(sc_norm_g), sb_norm_g=row(sb_norm_g),
            w_mix_out=w_mix_out[l].astype(BF16), norm_mem_g=row(norm_mem_g),
            w_mq=w_mq[l].astype(BF16), mq_norm_g=row(mq_norm_g), w_mo=w_mo[l].astype(BF16),
            norm_ffn_g=row(norm_ffn_g), w_gate=w_gate[l].astype(BF16), w_up=w_up[l].astype(BF16),
            ffn_conv_w=ffn_conv_w[l], w_down=w_down[l].astype(BF16))

        mk, mv = _mem_kv(mem, row(mem_in_norm_g), w_mk[l].astype(BF16), w_mv[l].astype(BF16),
                         row(mk_norm_g), _largest_tile(bp * n_mem, 512))
        mk3 = mk.reshape(bp, n_mem, MEM_HEADS * MEM_DH)
        mv3 = mv.reshape(bp, n_mem, MEM_HEADS * MEM_DH)
        xp, s_n, gc_n, sc_n, k_n, v_n, fc_n = _layer(
            xp, bp, lw, mk3, mv3, jnp.zeros((bp, GDN_HEADS, GDN_DK, GDN_DK), F32),
            zeros(bp, GDN_CONV_DIM), zeros(bp, SC_DIM), zeros(bp, d_ff), None, None)
        for i, a in enumerate((s_n, gc_n, sc_n, k_n.reshape(bp, tp, SB_HEADS, SB_DH),
                               v_n.reshape(bp, tp, SB_HEADS, SB_DH),
                               mk.reshape(bp, n_mem, MEM_HEADS, MEM_DH),
                               mv.reshape(bp, n_mem, MEM_HEADS, MEM_DH), fc_n)):
            outs[i].append(a)

        p_len = cache_sb_k.shape[2]
        xs, s_n, gc_n, sc_n, k_n, v_n, fc_n = _layer(
            xs, bs, lw,
            cache_mem_k[l].reshape(bs, n_mem, MEM_HEADS * MEM_DH),
            cache_mem_v[l].reshape(bs, n_mem, MEM_HEADS * MEM_DH),
            state_gdn[l], _pad_hist(cache_gdn_conv[l]), _pad_hist(cache_sc_conv[l]),
            _pad_hist(cache_ffn_conv[l]),
            cache_sb_k[l].reshape(bs, p_len, SB_W), cache_sb_v[l].reshape(bs, p_len, SB_W))
        for i, a in enumerate((s_n, gc_n, sc_n, k_n.reshape(bs, ts, SB_HEADS, SB_DH),
                               v_n.reshape(bs, ts, SB_HEADS, SB_DH), fc_n)):
            outs[8 + i].append(a)

    return (xp.reshape(bp, tp, d), xs.reshape(bs, ts, d)) + tuple(jnp.stack(o) for o in outs)
```

```python
import functools

import jax
import jax.numpy as jnp
from jax import lax
from jax.experimental import pallas as pl
from jax.experimental.pallas import tpu as pltpu

F32 = jnp.float32
BF16 = jnp.bfloat16
HI = lax.Precision.HIGHEST
EPS = 1e-6

GDN_HEADS = 4
GDN_DK = 128
GDN_QK = 512
GDN_CONV_DIM = 1536
SC_DIM = 256
SB_HEADS = 4
SB_DH = 64
SB_W = 256
MEM_HEADS = 4
MEM_DH = 128
CHUNK = 64
HIST = 8
FF_CHUNK = 1408
SB_LOG_ZERO = -104.0
VMEM_LIMIT = 56 * 1024 * 1024

C_QKV, C_GZ, C_GAB, C_SBCX, C_CQ, C_CK = 0, 1536, 2048, 2176, 2944, 3200


def _dot(a, b, precision=None):
    return jnp.dot(a, b, preferred_element_type=F32, precision=precision)


def _dot_nt(a, b, precision=None):
    return lax.dot_general(a, b, (((1,), (1,)), ((), ())),
                           preferred_element_type=F32, precision=precision)


def _dot_tn(a, b, precision=None):
    return lax.dot_general(a, b, (((0,), (0,)), ((), ())),
                           preferred_element_type=F32, precision=precision)


def _rms(x, g):
    return x * lax.rsqrt(jnp.mean(x * x, axis=-1, keepdims=True) + EPS) * g


def _sigmoid(x):
    return 1.0 / (1.0 + jnp.exp(-x))


def _silu(x):
    return x * _sigmoid(x)


def _softplus(x):
    return jnp.maximum(x, 0.0) + jnp.log(1.0 + jnp.exp(-jnp.abs(x)))


def _shift_rows(x, hist, d):
    r = pltpu.roll(x, d, 0)
    row = lax.broadcasted_iota(jnp.int32, hist.shape, 0)
    top = jnp.where(row < d, pltpu.roll(hist, d, 0), r[0:HIST])
    return jnp.concatenate([top, r[HIST:]], axis=0)


def _group_mean_matrix(width, group):
    r = lax.broadcasted_iota(jnp.int32, (width, width), 0) // group
    c = lax.broadcasted_iota(jnp.int32, (width, width), 1) // group
    return jnp.where(r == c, 1.0 / group, 0.0).astype(F32)


def _params(n_axes):
    return pltpu.CompilerParams(dimension_semantics=("arbitrary",) * n_axes,
                                vmem_limit_bytes=VMEM_LIMIT)


def _const_spec(shape):
    nd = len(shape)
    return pl.BlockSpec(shape, lambda *_: (0,) * nd, pipeline_mode=pl.Buffered(1))


def _in_proj_kernel(*refs, transposed_kv):
    if transposed_kv:
        (x_ref, g_ref, w_ref, wkv_ref, _, _, qkv_ref, gz_ref, gab_ref, sbcx_ref, cq_ref,
         kt_ref, vt_ref, ktb_ref, vtb_ref) = refs
    else:
        (x_ref, g_ref, w_ref, wkv_ref, qkv_ref, gz_ref, gab_ref, sbcx_ref, cq_ref,
         ck_ref, cv_ref) = refs
    h = _rms(x_ref[...], g_ref[...]).astype(BF16)

    def proj(c0, c1):
        return _dot(h, w_ref[:, c0:c1])

    for c in range(0, GDN_CONV_DIM, 512):
        qkv_ref[:, c:c + 512] = proj(C_QKV + c, C_QKV + c + 512)
    gz_ref[...] = proj(C_GZ, C_GAB)
    gab_ref[...] = proj(C_GAB, C_SBCX)
    for c in range(0, 3 * SC_DIM, 256):
        sbcx_ref[:, c:c + 256] = proj(C_SBCX + c, C_SBCX + c + 256)
    cq_ref[...] = (proj(C_CQ, C_CK) * (SB_DH ** -0.5)).astype(BF16)
    if transposed_kv:
        kt = _dot_nt(wkv_ref[0:SB_W, :], h)
        kt_ref[...] = kt
        ktb_ref[...] = kt.astype(BF16)
        vt = _dot_nt(wkv_ref[SB_W:2 * SB_W, :], h)
        vt_ref[...] = vt
        vtb_ref[...] = vt.astype(BF16)
    else:
        ck_ref[...] = _dot_nt(h, wkv_ref[0:SB_W, :])
        cv_ref[...] = _dot_nt(h, wkv_ref[SB_W:2 * SB_W, :])


def _in_proj_outs(n):
    return [jax.ShapeDtypeStruct((n, GDN_CONV_DIM), F32), jax.ShapeDtypeStruct((n, GDN_QK), F32),
            jax.ShapeDtypeStruct((n, 128), F32), jax.ShapeDtypeStruct((n, 3 * SC_DIM), F32),
            jax.ShapeDtypeStruct((n, SB_W), BF16)]


def _in_proj_t(x, g, w, wkv_t, kt_all, vt_all, layer, batch, tm):
    n, d = x.shape
    t_len = n // batch
    nt = t_len // tm
    row = lambda width: pl.BlockSpec((tm, width), lambda b, t: (b * nt + t, 0))
    stacked = pl.BlockSpec((None, None, SB_W, tm), lambda b, t: (layer, b, 0, t))
    per_layer = pl.BlockSpec((None, SB_W, tm), lambda b, t: (b, 0, t))
    any_spec = pl.BlockSpec(memory_space=pl.ANY)
    out_shape = _in_proj_outs(n) + [
        jax.ShapeDtypeStruct(kt_all.shape, F32), jax.ShapeDtypeStruct(vt_all.shape, F32),
        jax.ShapeDtypeStruct((batch, SB_W, t_len), BF16), jax.ShapeDtypeStruct((batch, SB_W, t_len), BF16)]
    return pl.pallas_call(
        functools.partial(_in_proj_kernel, transposed_kv=True),
        out_shape=tuple(out_shape),
        grid=(batch, nt),
        in_specs=[row(d), _const_spec((1, d)), _const_spec(w.shape), _const_spec(wkv_t.shape),
                  any_spec, any_spec],
        out_specs=(row(GDN_CONV_DIM), row(GDN_QK), row(128), row(3 * SC_DIM), row(SB_W),
                   stacked, stacked, per_layer, per_layer),
        input_output_aliases={4: 5, 5: 6},
        compiler_params=_params(2),
        name="in_proj",
    )(x, g, w, wkv_t, kt_all, vt_all)


def _in_proj(x, g, w, wkv_t, tm):
    n, d = x.shape
    row = lambda width: pl.BlockSpec((tm, width), lambda i: (i, 0))
    out_shape = _in_proj_outs(n) + [jax.ShapeDtypeStruct((n, SB_W), F32)] * 2
    return pl.pallas_call(
        functools.partial(_in_proj_kernel, transposed_kv=False),
        out_shape=tuple(out_shape),
        grid=(n // tm,),
        in_specs=[row(d), _const_spec((1, d)), _const_spec(w.shape), _const_spec(wkv_t.shape)],
        out_specs=(row(GDN_CONV_DIM), row(GDN_QK), row(128), row(3 * SC_DIM), row(SB_W),
                   row(SB_W), row(SB_W)),
        compiler_params=_params(1),
        name="in_proj",
    )(x, g, w, wkv_t)


def _split2(x):
    hi = x.astype(BF16)
    return hi, (x - hi.astype(F32)).astype(BF16)


def _dot2(a, b_exact):
    hi, lo = _split2(a)
    return _dot(hi, b_exact) + _dot(lo, b_exact)


def _gdn_kernel(qkv_ref, gz_ref, gab_ref, sbcx_ref, hg_ref, hs_ref, s0_ref,
                cw_ref, scw_ref, gp_ref, gng_ref, sng_ref,
                ya_ref, yb_ref, s_ref, tg_ref, ts_ref, extg, exts, *, tt, chunk):
    t = pl.program_id(1)
    nh = GDN_HEADS
    sw = nh * chunk
    sec = max(sw, 128)
    n_chunks = tt // chunk

    @pl.when(t == 0)
    def _():
        extg[...] = hg_ref[...]
        exts[...] = hs_ref[...]
        s_ref[...] = s0_ref[...]

    x = qkv_ref[...]
    hist = extg[...]
    conv = (_shift_rows(x, hist, 3) * cw_ref[0:1, :] + _shift_rows(x, hist, 2) * cw_ref[1:2, :]
            + _shift_rows(x, hist, 1) * cw_ref[2:3, :] + x * cw_ref[3:4, :])
    extg[...] = x[tt - HIST:tt, :]
    tg_ref[...] = x[tt - HIST:tt, :]
    act = _silu(conv)

    def l2n(a):
        parts = []
        for h in range(nh):
            ah = a[:, GDN_DK * h:GDN_DK * (h + 1)]
            parts.append(ah * lax.rsqrt(jnp.sum(ah * ah, axis=-1, keepdims=True) + EPS))
        return jnp.concatenate(parts, axis=1)

    qn = l2n(act[:, 0:GDN_QK]) * (GDN_DK ** -0.5)
    kn = l2n(act[:, GDN_QK:2 * GDN_QK])
    vv = act[:, 2 * GDN_QK:3 * GDN_QK]

    slab = gab_ref[...]
    beta_s = _sigmoid(slab)
    g_s = -jnp.exp(gp_ref[0:1, :]) * _softplus(slab + gp_ref[1:2, :])
    ri = lax.broadcasted_iota(jnp.int32, (tt, tt), 0)
    ci = lax.broadcasted_iota(jnp.int32, (tt, tt), 1)
    tri = jnp.where(jnp.logical_and(ri // chunk == ci // chunk, ri >= ci), 1.0, 0.0).astype(BF16)
    g_hi, g_lo = _split2(g_s)
    gc = _dot(tri, g_hi) + _dot(tri, g_lo)
    lane128 = lax.broadcasted_iota(jnp.int32, (1, 128), 1)
    slab2 = jnp.where(lane128 < nh, beta_s, gc)
    ew = 2 * GDN_QK + 2 * sec
    er = lax.broadcasted_iota(jnp.int32, (128, ew), 0)
    ec = lax.broadcasted_iota(jnp.int32, (128, ew), 1)
    src_row = jnp.where(ec < GDN_QK, ec // GDN_DK,
              jnp.where(ec < 2 * GDN_QK, nh + (ec - GDN_QK) // GDN_DK,
              jnp.where(ec < 2 * GDN_QK + sec, (ec - 2 * GDN_QK) // chunk,
                        nh + (ec - 2 * GDN_QK - sec) // chunk)))
    expand = jnp.where(er == src_row, 1.0, 0.0).astype(BF16)
    xp = _dot2(slab2, expand)
    b512 = xp[:, 0:GDN_QK]
    gc512 = xp[:, GDN_QK:2 * GDN_QK]
    b_s = xp[:, 2 * GDN_QK:2 * GDN_QK + sec][:, 0:sw]
    gc_s = xp[:, 2 * GDN_QK + sec:2 * GDN_QK + 2 * sec][:, 0:sw]
    eg512 = jnp.exp(gc512)

    qb = qn.astype(BF16)
    kb = kn.astype(BF16)
    qe = (qn * eg512).astype(BF16)
    rv = vv * b512
    rk = kn * (b512 * eg512)

    r_s = lax.broadcasted_iota(jnp.int32, (chunk, sw), 0)
    c_s = lax.broadcasted_iota(jnp.int32, (chunk, sw), 1)
    c_in = c_s - (c_s // chunk) * chunk
    causal_s = r_s >= c_in
    strict_s = r_s > c_in
    eye_s = jnp.where(r_s == c_in, 1.0, 0.0).astype(F32)
    pair_mask = jnp.logical_and(r_s // 2 == c_in // 2, strict_s)
    level_masks = []
    m = 2
    while m < chunk:
        level_masks.append(jnp.logical_and(
            jnp.logical_and(r_s // (2 * m) == c_in // (2 * m), r_s // m != c_in // m), strict_s))
        m *= 2
    bd_r = lax.broadcasted_iota(jnp.int32, (sw, sw), 0) // chunk
    bd_c = lax.broadcasted_iota(jnp.int32, (sw, sw), 1) // chunk
    bd_mask = bd_r == bd_c
    bdw_r = lax.broadcasted_iota(jnp.int32, (sw, GDN_QK), 0) // chunk
    bdw_c = lax.broadcasted_iota(jnp.int32, (sw, GDN_QK), 1) // GDN_DK
    bdw_mask = bdw_r == bdw_c
    zero_b = jnp.zeros((), BF16)
    ones_cc = jnp.ones((chunk, chunk), BF16)

    def bd(a):
        return jnp.where(bd_mask, jnp.concatenate([a] * nh, axis=0), zero_b)

    def bdw(a):
        return jnp.where(bdw_mask, jnp.concatenate([a] * nh, axis=0), zero_b)

    lsplit, inv, qkd = [], [], []
    for c in range(n_chunks):
        rows = slice(c * chunk, (c + 1) * chunk)
        k_bd = bdw(kb[rows])
        kk = _dot_nt(kb[rows], k_bd)
        qk = _dot_nt(qb[rows], k_bd)
        gcs = gc_s[rows]
        g_hi, g_lo = _split2(eye_s * gcs)
        row = _dot(ones_cc, g_hi) + _dot(ones_cc, g_lo)
        decay = jnp.exp(jnp.where(causal_s, gcs - row, -jnp.inf))
        lm = jnp.where(strict_s, b_s[rows] * kk * decay, 0.0)
        lsplit.append(_split2(lm))
        inv.append(eye_s - jnp.where(pair_mask, lm, 0.0))
        qkd.append((qk * decay).astype(BF16))

    for mask in level_masks:
        for c in range(n_chunks):
            c_hi = jnp.where(mask, lsplit[c][0], zero_b)
            c_lo = jnp.where(mask, lsplit[c][1], zero_b)
            x_hi, x_lo = _split2(inv[c])
            bx_hi = bd(x_hi)
            cx = _dot(c_hi, bx_hi) + (_dot(c_hi, bd(x_lo)) + _dot(c_lo, bx_hi))
            t_hi, t_lo = _split2(cx)
            bt_hi = bd(t_hi)
            inv[c] = inv[c] - (_dot(x_hi, bt_hi) + (_dot(x_hi, bd(t_lo)) + _dot(x_lo, bt_hi)))

    u_all, w_all = [], []
    for c in range(n_chunks):
        rows = slice(c * chunk, (c + 1) * chunk)
        x_b = inv[c].astype(BF16)
        u_all.append(_dot(x_b, bdw(rv[rows].astype(BF16))))
        w_all.append(_dot(x_b, bdw(rk[rows].astype(BF16))).astype(BF16))

    gng = gng_ref[...]
    for c in range(n_chunks):
        rows = slice(c * chunk, (c + 1) * chunk)
        g_last = gc512[(c + 1) * chunk - 1:(c + 1) * chunk, :]
        kd = (kn[rows] * jnp.exp(g_last - gc512[rows])).astype(BF16)
        eg_last = jnp.exp(g_last)
        s_old = [s_ref[h] for h in range(nh)]
        s_b = [s.astype(BF16) for s in s_old]
        v_new = jnp.concatenate(
            [u_all[c][:, GDN_DK * h:GDN_DK * (h + 1)]
             - _dot(w_all[c][:, GDN_DK * h:GDN_DK * (h + 1)], s_b[h]) for h in range(nh)], axis=1)
        vb = v_new.astype(BF16)
        o_state = jnp.concatenate(
            [_dot(qe[rows][:, GDN_DK * h:GDN_DK * (h + 1)], s_b[h]) for h in range(nh)], axis=1)
        o = o_state + _dot(qkd[c], bdw(vb))
        for h in range(nh):
            lanes = slice(GDN_DK * h, GDN_DK * (h + 1))
            s_ref[h] = s_old[h] * eg_last[:, lanes] + _dot_tn(kd[:, lanes], vb[:, lanes])
        z = gz_ref[c * chunk:(c + 1) * chunk, :]
        ya_ref[c * chunk:(c + 1) * chunk, :] = (jnp.concatenate(
            [_rms(o[:, GDN_DK * h:GDN_DK * (h + 1)], gng) for h in range(nh)], axis=1)
            * _silu(z)).astype(BF16)

    sbcx = sbcx_ref[...]
    s_b = sbcx[:, 0:SC_DIM]
    pre = sbcx[:, SC_DIM:2 * SC_DIM] * sbcx[:, 2 * SC_DIM:3 * SC_DIM]
    hist_s = exts[...]
    u_c = (_shift_rows(pre, hist_s, 2) * scw_ref[0:1, :] + _shift_rows(pre, hist_s, 1) * scw_ref[1:2, :]
           + pre * scw_ref[2:3, :])
    exts[...] = pre[tt - HIST:tt, :]
    ts_ref[...] = pre[tt - HIST:tt, :]
    yb = s_b * u_c
    ms = _dot(yb * yb, _group_mean_matrix(SC_DIM, SC_DIM // 4), HI)
    yb_ref[...] = (yb * lax.rsqrt(ms + EPS) * sng_ref[...]).astype(BF16)


def _gdn(qkv, gz, gab, sbcx, hist_g, hist_s, s0, cw, scw, gp, gng, sng, batch, tt, chunk):
    n = qkv.shape[0]
    nt = n // batch // tt
    row = lambda width: pl.BlockSpec((tt, width), lambda b, t: (b * nt + t, 0))
    per_b3 = lambda width: pl.BlockSpec((None, HIST, width), lambda b, t: (b, 0, 0))
    state = pl.BlockSpec((None, GDN_HEADS, GDN_DK, GDN_DK), lambda b, t: (b, 0, 0, 0))
    out_shape = (
        jax.ShapeDtypeStruct((n, GDN_QK), BF16),
        jax.ShapeDtypeStruct((n, SC_DIM), BF16),
        jax.ShapeDtypeStruct((batch, GDN_HEADS, GDN_DK, GDN_DK), F32),
        jax.ShapeDtypeStruct((batch, HIST, GDN_CONV_DIM), F32),
        jax.ShapeDtypeStruct((batch, HIST, SC_DIM), F32),
    )
    return pl.pallas_call(
        functools.partial(_gdn_kernel, tt=tt, chunk=chunk),
        out_shape=out_shape,
        grid=(batch, nt),
        in_specs=[row(GDN_CONV_DIM), row(GDN_QK), row(128), row(3 * SC_DIM),
                  per_b3(GDN_CONV_DIM), per_b3(SC_DIM), state,
                  _const_spec(cw.shape), _const_spec(scw.shape), _const_spec(gp.shape),
                  _const_spec(gng.shape), _const_spec(sng.shape)],
        out_specs=(row(GDN_QK), row(SC_DIM), state, per_b3(GDN_CONV_DIM), per_b3(SC_DIM)),
        scratch_shapes=[pltpu.VMEM((HIST, GDN_CONV_DIM), F32), pltpu.VMEM((HIST, SC_DIM), F32)],
        compiler_params=_params(2),
        name="gdn",
    )(qkv, gz, gab, sbcx, hist_g, hist_s, s0, cw, scw, gp, gng, sng)


def _sb_kernel(q_ref, kd_ref, vd_ref, kp_ref, vp_ref, g_ref, y_ref, acc_ref, c_ref,
               *, tq, tkd, tk, n_past_static):
    lane = lax.broadcasted_iota(jnp.int32, (1, SB_W), 1) // SB_DH
    rowh = lax.broadcasted_iota(jnp.int32, (SB_W, 1), 0) // SB_DH
    q = q_ref[...]
    zero_b = jnp.zeros((), BF16)
    q_heads = [jnp.where(lane == h, q, zero_b) for h in range(SB_HEADS)]

    acc_ref[...] = jnp.zeros_like(acc_ref)
    c_ref[...] = jnp.zeros_like(c_ref)

    def later_matrix(width):
        r = lax.broadcasted_iota(jnp.int32, (width, width), 0)
        c = lax.broadcasted_iota(jnp.int32, (width, width), 1)
        return jnp.where(r > c, 1.0, 0.0).astype(BF16)

    def block(kt, vt, later, mask):
        pv = jnp.zeros((tq, SB_W), F32)
        for h in range(SB_HEADS):
            z = _dot(q_heads[h], kt)
            sp = _softplus(z)
            l1 = -sp if mask is None else jnp.where(mask, -sp, 0.0)
            l1_hi, l1_lo = _split2(l1)
            rc = _dot(l1_hi, later) + _dot(l1_lo, later)
            carry = c_ref[:, h:h + 1]
            a = jnp.exp((z - sp) + rc + carry)
            if mask is not None:
                a = jnp.where(mask, a, 0.0)
            pv = pv + _dot_nt(a.astype(BF16), jnp.where(rowh == h, vt, zero_b))
            c_ref[:, h:h + 1] = carry + rc[:, 0:1] + l1[:, 0:1]
        acc_ref[...] += pv

    rq = lax.broadcasted_iota(jnp.int32, (tq, tkd), 0)
    ck = lax.broadcasted_iota(jnp.int32, (tq, tkd), 1)
    block(kd_ref[...], vd_ref[...], later_matrix(tkd), ck < rq)

    if n_past_static is None:
        n_past = pl.program_id(1) * (tq // tk)
    else:
        n_past = n_past_static
    later_past = later_matrix(tk)

    def cond(state):
        j, live = state
        return jnp.logical_and(j >= 0, live)

    def body(state):
        j, _ = state
        start = pl.multiple_of(j * tk, tk)
        kt = kp_ref[:, pl.ds(start, tk)].astype(BF16)
        vt = vp_ref[:, pl.ds(start, tk)].astype(BF16)
        block(kt, vt, later_past, None)
        live = jnp.max(c_ref[:, 0:SB_HEADS]) > SB_LOG_ZERO
        return j - 1, live

    lax.while_loop(cond, body, (n_past - 1, jnp.bool_(True)))

    o = acc_ref[...]
    ms = _dot(o * o, _group_mean_matrix(SB_W, SB_DH), HI)
    y_ref[...] = (o * lax.rsqrt(ms + EPS) * g_ref[...]).astype(BF16)


def _sb_attn(q, kd, vd, kp, vp, g, batch, tq, tkd, tk, n_past_static, layer=None):
    n = q.shape[0]
    nq = n // batch // tq
    qrow = pl.BlockSpec((tq, SB_W), lambda b, i: (b * nq + i, 0))
    if kd.shape[2] == tkd:
        diag = pl.BlockSpec((None, SB_W, tkd), lambda b, i: (b, 0, 0))
    else:
        diag = pl.BlockSpec((None, SB_W, tkd), lambda b, i: (b, 0, i))
    if layer is None:
        past = pl.BlockSpec((None, SB_W, kp.shape[2]), lambda b, i: (b, 0, 0))
    else:
        past = pl.BlockSpec((None, None, SB_W, kp.shape[3]), lambda b, i: (layer, b, 0, 0))
    return pl.pallas_call(
        functools.partial(_sb_kernel, tq=tq, tkd=tkd, tk=tk, n_past_static=n_past_static),
        out_shape=jax.ShapeDtypeStruct((n, SB_W), BF16),
        grid=(batch, nq),
        in_specs=[qrow, diag, diag, past, past, _const_spec(g.shape)],
        out_specs=qrow,
        scratch_shapes=[pltpu.VMEM((tq, SB_W), F32), pltpu.VMEM((tq, 128), F32)],
        compiler_params=_params(2),
        name="sb_attn",
    )(q, kd, vd, kp, vp, g)


def _mixmem_kernel(x_ref, ya_ref, yb_ref, yc_ref, wmix_ref, g_ref, wq_ref, qg_ref,
                   mk_ref, mv_ref, wo_ref, o_ref):
    x1 = (x_ref[...]
          + _dot(ya_ref[...], wmix_ref[0:GDN_QK, :])
          + _dot(yb_ref[...], wmix_ref[GDN_QK:GDN_QK + SC_DIM, :])
          + _dot(yc_ref[...], wmix_ref[GDN_QK + SC_DIM:GDN_QK + SC_DIM + SB_W, :]))
    h = _rms(x1, g_ref[...]).astype(BF16)
    qm = _dot(h, wq_ref[...])
    qg = qg_ref[...]
    outs = []
    for hd in range(MEM_HEADS):
        lanes = slice(MEM_DH * hd, MEM_DH * (hd + 1))
        qh = _rms(qm[:, lanes], qg).astype(BF16)
        kh = mk_ref[:, lanes].astype(BF16)
        vh = mv_ref[:, lanes].astype(BF16)
        s = _dot_nt(qh, kh) * (MEM_DH ** -0.5)
        e = jnp.exp(s - jnp.max(s, axis=-1, keepdims=True))
        p = e / jnp.sum(e, axis=-1, keepdims=True)
        outs.append(_dot(p.astype(BF16), vh).astype(BF16))
    om = jnp.concatenate(outs, axis=1)
    o_ref[...] = x1 + _dot(om, wo_ref[...])


def _mixmem(x, ya, yb, yc, wmix, g, wq, qg, mk, mv, wo, batch, tm):
    n, d = x.shape
    nt = n // batch // tm
    row = lambda width: pl.BlockSpec((tm, width), lambda b, t: (b * nt + t, 0))
    mem = pl.BlockSpec((None,) + mk.shape[1:], lambda b, t: (b, 0, 0))
    return pl.pallas_call(
        _mixmem_kernel,
        out_shape=jax.ShapeDtypeStruct((n, d), F32),
        grid=(batch, nt),
        in_specs=[row(d), row(GDN_QK), row(SC_DIM), row(SB_W),
                  _const_spec(wmix.shape), _const_spec(g.shape), _const_spec(wq.shape),
                  _const_spec(qg.shape), mem, mem, _const_spec(wo.shape)],
        out_specs=row(d),
        compiler_params=_params(2),
        name="mixmem",
    )(x, ya, yb, yc, wmix, g, wq, qg, mk, mv, wo)


def _ffn_kernel(x_ref, g_ref, wg_ref, wu_ref, cw_ref, wd_ref, hist_ref, o_ref, tail_ref, *, tm, d_ff):
    @pl.when(pl.program_id(1) == 0)
    def _():
        tail_ref[...] = hist_ref[...]

    x = x_ref[...]
    h = _rms(x, g_ref[...]).astype(BF16)
    acc = x
    for c0 in range(0, d_ff, FF_CHUNK):
        cols = slice(c0, c0 + FF_CHUNK)
        gate = _dot(h, wg_ref[:, cols])
        hist = tail_ref[:, cols]
        conv = (_shift_rows(gate, hist, 2) * cw_ref[0:1, cols]
                + _shift_rows(gate, hist, 1) * cw_ref[1:2, cols] + gate * cw_ref[2:3, cols])
        tail_ref[:, cols] = gate[tm - HIST:tm, :]
        up = _dot(h, wu_ref[:, cols])
        hid = (_silu(conv) * up).astype(BF16)
        acc = acc + _dot(hid, wd_ref[cols, :])
    o_ref[...] = acc


def _ffn(x, g, wg, wu, cw, wd, hist, batch, tm):
    n, d = x.shape
    d_ff = wg.shape[1]
    nt = n // batch // tm
    row = pl.BlockSpec((tm, d), lambda b, t: (b * nt + t, 0))
    per_b = pl.BlockSpec((None, HIST, d_ff), lambda b, t: (b, 0, 0))
    return pl.pallas_call(
        functools.partial(_ffn_kernel, tm=tm, d_ff=d_ff),
        out_shape=(jax.ShapeDtypeStruct((n, d), F32),
                   jax.ShapeDtypeStruct((batch, HIST, d_ff), F32)),
        grid=(batch, nt),
        in_specs=[row, _const_spec(g.shape), _const_spec(wg.shape), _const_spec(wu.shape),
                  _const_spec(cw.shape), _const_spec(wd.shape), per_b],
        out_specs=(row, per_b),
        compiler_params=_params(2),
        name="ffn",
    )(x, g, wg, wu, cw, wd, hist)


def _mem_kv_kernel(m_ref, g_ref, wk_ref, wv_ref, kg_ref, k_ref, v_ref):
    m = _rms(m_ref[...], g_ref[...]).astype(BF16)
    k = _dot(m, wk_ref[...])
    kg = kg_ref[...]
    for hd in range(MEM_HEADS):
        lanes = slice(MEM_DH * hd, MEM_DH * (hd + 1))
        k_ref[:, lanes] = _rms(k[:, lanes], kg)
    v_ref[...] = _dot(m, wv_ref[...])


def _mem_kv(mem, g, wk, wv, kg, tm):
    n, d = mem.shape
    w = wk.shape[1]
    row = lambda width: pl.BlockSpec((tm, width), lambda i: (i, 0))
    return pl.pallas_call(
        _mem_kv_kernel,
        out_shape=(jax.ShapeDtypeStruct((n, w), F32), jax.ShapeDtypeStruct((n, w), F32)),
        grid=(n // tm,),
        in_specs=[row(d), _const_spec(g.shape), _const_spec(wk.shape), _const_spec(wv.shape),
                  _const_spec(kg.shape)],
        out_specs=(row(w), row(w)),
        compiler_params=_params(1),
        name="mem_kv",
    )(mem, g, wk, wv, kg)


def _pad_hist(buf):
    return jnp.pad(buf, ((0, 0), (HIST - buf.shape[1], 0), (0, 0)))


def _largest_tile(total, cap):
    t = min(total, cap)
    while total % t:
        t //= 2
    return t


def _layer(x, batch, lw, mem_k, mem_v, s0, hist_g, hist_s, hist_f, layer,
           kv_stack=None, sb_cache=None):
    n = x.shape[0]
    t_len = n // batch
    chunk = min(CHUNK, t_len)
    tt = _largest_tile(t_len, 4 * chunk)
    tm = _largest_tile(t_len, 512)

    if kv_stack is not None:
        qkv, gz, gab, sbcx, cq, kt_all, vt_all, ktb, vtb = _in_proj_t(
            x, lw["norm_mix_g"], lw["w_in"], lw["w_kv_t"], kv_stack[0], kv_stack[1], layer, batch, tm)
        kv_out = (kt_all, vt_all)
    else:
        qkv, gz, gab, sbcx, cq, ck, cv = _in_proj(
            x, lw["norm_mix_g"], lw["w_in"], lw["w_kv_t"], _largest_tile(n, 512))
        kv_out = (ck, cv)

    ya, yb, s_new, tail_g, tail_s = _gdn(
        qkv, gz, gab, sbcx, hist_g, hist_s, s0, lw["gdn_conv_w"], lw["sc_conv_w"],
        lw["gate_params"], lw["gdn_norm_g"], lw["sc_norm_g"], batch, tt, chunk)

    if kv_stack is not None:
        tq = _largest_tile(t_len, 256)
        yc = _sb_attn(cq, ktb, vtb, ktb, vtb, lw["sb_norm_g"], batch, tq, tq, tq, None)
    else:
        tkd = 128
        new_t = lambda a: jnp.pad(jnp.swapaxes(a.reshape(batch, t_len, SB_W), 1, 2).astype(BF16),
                                  ((0, 0), (0, 0), (0, tkd - t_len)))
        p_len = sb_cache[0].shape[3]
        tk = _largest_tile(p_len, 256)
        yc = _sb_attn(cq, new_t(ck), new_t(cv), sb_cache[0], sb_cache[1], lw["sb_norm_g"],
                      batch, t_len, tkd, tk, p_len // tk, layer=layer)

    x = _mixmem(x, ya, yb, yc, lw["w_mix_out"], lw["norm_mem_g"], lw["w_mq"], lw["mq_norm_g"],
                mem_k, mem_v, lw["w_mo"], batch, tm)
    x, tail_f = _ffn(x, lw["norm_ffn_g"], lw["w_gate"], lw["w_up"], lw["ffn_conv_w"],
                     lw["w_down"], hist_f, batch, tm)
    return x, s_new, tail_g[:, HIST - 3:], tail_s[:, HIST - 2:], kv_out, tail_f[:, HIST - 2:]


def kernel(x_prompt, x_sample, mem_prompt, state_gdn, cache_gdn_conv, cache_sc_conv, cache_sb_k, cache_sb_v, cache_mem_k, cache_mem_v, cache_ffn_conv, norm_mix_g, w_in, gdn_conv_w, gdn_A_log, gdn_dt_bias, gdn_norm_g, sc_conv_w, sc_norm_g, sb_norm_g, w_mix_out, norm_mem_g, mem_in_norm_g, w_mq, w_mk, w_mv, mq_norm_g, mk_norm_g, w_mo, norm_ffn_g, w_gate, w_up, ffn_conv_w, w_down):
    depth = w_in.shape[0]
    bp, tp, d = x_prompt.shape
    bs, ts, _ = x_sample.shape
    n_mem = mem_prompt.shape[1]
    d_ff = w_gate.shape[2]
    p_len = cache_sb_k.shape[2]
    n_gate = 2 * GDN_QK + 2 * GDN_QK
    n_kv = w_in.shape[2] - 2 * SB_W

    xp = x_prompt.reshape(bp * tp, d)
    xs = x_sample.reshape(bs * ts, d)
    mem = mem_prompt.reshape(bp * n_mem, d)
    zeros = lambda b, c: jnp.zeros((b, HIST, c), F32)
    to_t = lambda c: jnp.transpose(c, (0, 1, 3, 4, 2)).reshape(depth, bs, SB_W, p_len)
    sb_cache = (to_t(cache_sb_k), to_t(cache_sb_v))
    kv_stack = (jnp.zeros((depth, bp, SB_W, tp), F32), jnp.zeros((depth, bp, SB_W, tp), F32))

    outs = [[] for _ in range(14)]
    for l in range(depth):
        w = w_in[l]
        w_packed = jnp.concatenate(
            [w[:, :n_gate + 2 * GDN_HEADS],
             jnp.zeros((d, 128 - 2 * GDN_HEADS), w.dtype),
             w[:, n_gate + 2 * GDN_HEADS:n_kv]], axis=1).astype(BF16)
        gate_params = jnp.zeros((8, 128), F32)
        gate_params = gate_params.at[0, GDN_HEADS:2 * GDN_HEADS].set(gdn_A_log[l])
        gate_params = gate_params.at[1, GDN_HEADS:2 * GDN_HEADS].set(gdn_dt_bias[l])
        row = lambda a: a[l].reshape(1, -1)
        lw = dict(
            norm_mix_g=row(norm_mix_g), w_in=w_packed, w_kv_t=w[:, n_kv:].T.astype(BF16),
            gdn_conv_w=gdn_conv_w[l],
            gate_params=gate_params, gdn_norm_g=row(gdn_norm_g), sc_conv_w=sc_conv_w[l],
            sc_norm_g=row(sc_norm_g), sb_norm_g=row(sb_norm_g),
            w_mix_out=w_mix_out[l].astype(BF16), norm_mem_g=row(norm_mem_g),
            w_mq=w_mq[l].astype(BF16), mq_norm_g=row(mq_norm_g), w_mo=w_mo[l].astype(BF16),
            norm_ffn_g=row(norm_ffn_g), w_gate=w_gate[l].astype(BF16), w_up=w_up[l].astype(BF16),
            ffn_conv_w=ffn_conv_w[l], w_down=w_down[l].astype(BF16))

        mk, mv = _mem_kv(mem, row(mem_in_norm_g), w_mk[l].astype(BF16), w_mv[l].astype(BF16),
                         row(mk_norm_g), _largest_tile(bp * n_mem, 512))
        mk3 = mk.reshape(bp, n_mem, MEM_HEADS * MEM_DH)
        mv3 = mv.reshape(bp, n_mem, MEM_HEADS * MEM_DH)
        xp, s_n, gc_n, sc_n, kv_stack, fc_n = _layer(
            xp, bp, lw, mk3, mv3, jnp.zeros((bp, GDN_HEADS, GDN_DK, GDN_DK), F32),
            zeros(bp, GDN_CONV_DIM), zeros(bp, SC_DIM), zeros(bp, d_ff), l, kv_stack=kv_stack)
        for i, a in ((0, s_n), (1, gc_n), (2, sc_n), (5, mk.reshape(bp, n_mem, MEM_HEADS, MEM_DH)),
                     (6, mv.reshape(bp, n_mem, MEM_HEADS, MEM_DH)), (7, fc_n)):
            outs[i].append(a)

        xs, s_n, gc_n, sc_n, (k_n, v_n), fc_n = _layer(
            xs, bs, lw,
            cache_mem_k[l].reshape(bs, n_mem, MEM_HEADS * MEM_DH),
            cache_mem_v[l].reshape(bs, n_mem, MEM_HEADS * MEM_DH),
            state_gdn[l], _pad_hist(cache_gdn_conv[l]), _pad_hist(cache_sc_conv[l]),
            _pad_hist(cache_ffn_conv[l]), l, sb_cache=sb_cache)
        for i, a in enumerate((s_n, gc_n, sc_n, k_n.reshape(bs, ts, SB_HEADS, SB_DH),
                               v_n.reshape(bs, ts, SB_HEADS, SB_DH), fc_n)):
            outs[8 + i].append(a)

    from_t = lambda a: jnp.transpose(a.reshape(depth, bp, SB_HEADS, SB_DH, tp), (0, 1, 4, 2, 3))
    stacked = [jnp.stack(o) if o else None for o in outs]
    stacked[3], stacked[4] = from_t(kv_stack[0]), from_t(kv_stack[1])
    return (xp.reshape(bp, tp, d), xs.reshape(bs, ts, d)) + tuple(stacked)
```

```python
import functools

import jax
import jax.numpy as jnp
from jax import lax
from jax.experimental import pallas as pl
from jax.experimental.pallas import tpu as pltpu

F32 = jnp.float32
BF16 = jnp.bfloat16
HI = lax.Precision.HIGHEST
EPS = 1e-6

GDN_HEADS = 4
GDN_DK = 128
GDN_QK = 512
GDN_CONV_DIM = 1536
SC_DIM = 256
SB_HEADS = 4
SB_DH = 64
SB_W = 256
MEM_HEADS = 4
MEM_DH = 128
CHUNK = 64
HIST = 8
FF_CHUNK = 1408
SB_LOG_ZERO = -104.0
VMEM_LIMIT = 56 * 1024 * 1024

C_QKV, C_GZ, C_GAB, C_SBCX, C_CQ, C_CK = 0, 1536, 2048, 2176, 2944, 3200


def _dot(a, b, precision=None):
    return jnp.dot(a, b, preferred_element_type=F32, precision=precision)


def _dot_nt(a, b, precision=None):
    return lax.dot_general(a, b, (((1,), (1,)), ((), ())),
                           preferred_element_type=F32, precision=precision)


def _dot_tn(a, b, precision=None):
    return lax.dot_general(a, b, (((0,), (0,)), ((), ())),
                           preferred_element_type=F32, precision=precision)


def _rms(x, g):
    return x * lax.rsqrt(jnp.mean(x * x, axis=-1, keepdims=True) + EPS) * g


def _sigmoid(x):
    return 1.0 / (1.0 + jnp.exp(-x))


def _silu(x):
    return x * _sigmoid(x)


def _softplus(x):
    return jnp.maximum(x, 0.0) + jnp.log(1.0 + jnp.exp(-jnp.abs(x)))


def _shift_rows(x, hist, d):
    r = pltpu.roll(x, d, 0)
    row = lax.broadcasted_iota(jnp.int32, hist.shape, 0)
    top = jnp.where(row < d, pltpu.roll(hist, d, 0), r[0:HIST])
    return jnp.concatenate([top, r[HIST:]], axis=0)


def _group_mean_matrix(width, group):
    r = lax.broadcasted_iota(jnp.int32, (width, width), 0) // group
    c = lax.broadcasted_iota(jnp.int32, (width, width), 1) // group
    return jnp.where(r == c, 1.0 / group, 0.0).astype(F32)


def _params(n_axes):
    return pltpu.CompilerParams(dimension_semantics=("arbitrary",) * n_axes,
                                vmem_limit_bytes=VMEM_LIMIT)


def _const_spec(shape):
    nd = len(shape)
    return pl.BlockSpec(shape, lambda *_: (0,) * nd, pipeline_mode=pl.Buffered(1))


def _l2norm_heads(a):
    parts = []
    for h in range(GDN_HEADS):
        ah = a[:, GDN_DK * h:GDN_DK * (h + 1)]
        parts.append(ah * lax.rsqrt(jnp.sum(ah * ah, axis=-1, keepdims=True) + EPS))
    return jnp.concatenate(parts, axis=1)


def _gdn_front(x, hist, cw_ref, part):
    cols = slice(GDN_QK * part, GDN_QK * (part + 1))
    conv = (_shift_rows(x, hist, 3) * cw_ref[0:1, cols] + _shift_rows(x, hist, 2) * cw_ref[1:2, cols]
            + _shift_rows(x, hist, 1) * cw_ref[2:3, cols] + x * cw_ref[3:4, cols])
    act = _silu(conv)
    if part == 0:
        return _l2norm_heads(act) * (GDN_DK ** -0.5)
    return _l2norm_heads(act) if part == 1 else act


def _short_conv(sbcx, hist, scw_ref, sng_ref):
    s_b = sbcx[:, 0:SC_DIM]
    pre = sbcx[:, SC_DIM:2 * SC_DIM] * sbcx[:, 2 * SC_DIM:3 * SC_DIM]
    u_c = (_shift_rows(pre, hist, 2) * scw_ref[0:1, :] + _shift_rows(pre, hist, 1) * scw_ref[1:2, :]
           + pre * scw_ref[2:3, :])
    yb = s_b * u_c
    ms = _dot(yb * yb, _group_mean_matrix(SC_DIM, SC_DIM // 4), HI)
    return (yb * lax.rsqrt(ms + EPS) * sng_ref[...]).astype(BF16), pre


def _in_proj_kernel(*refs, transposed_kv):
    if transposed_kv:
        (x_ref, g_ref, w_ref, wkv_ref, _, _, qkv_ref, gz_ref, gab_ref, sbcx_ref, cq_ref,
         kt_ref, vt_ref, ktb_ref, vtb_ref) = refs
    else:
        (x_ref, g_ref, w_ref, wkv_ref, qkv_ref, gz_ref, gab_ref, sbcx_ref, cq_ref,
         ck_ref, cv_ref) = refs
    h = _rms(x_ref[...], g_ref[...]).astype(BF16)

    def proj(c0, c1):
        return _dot(h, w_ref[:, c0:c1])

    for c in range(0, GDN_CONV_DIM, 512):
        qkv_ref[:, c:c + 512] = proj(C_QKV + c, C_QKV + c + 512)
    gz_ref[...] = proj(C_GZ, C_GAB)
    gab_ref[...] = proj(C_GAB, C_SBCX)
    for c in range(0, 3 * SC_DIM, 256):
        sbcx_ref[:, c:c + 256] = proj(C_SBCX + c, C_SBCX + c + 256)
    cq_ref[...] = (proj(C_CQ, C_CK) * (SB_DH ** -0.5)).astype(BF16)
    if transposed_kv:
        kt = _dot_nt(wkv_ref[0:SB_W, :], h)
        kt_ref[...] = kt
        ktb_ref[...] = kt.astype(BF16)
        vt = _dot_nt(wkv_ref[SB_W:2 * SB_W, :], h)
        vt_ref[...] = vt
        vtb_ref[...] = vt.astype(BF16)
    else:
        ck_ref[...] = _dot_nt(h, wkv_ref[0:SB_W, :])
        cv_ref[...] = _dot_nt(h, wkv_ref[SB_W:2 * SB_W, :])


def _in_proj_outs(n):
    return [jax.ShapeDtypeStruct((n, GDN_CONV_DIM), F32), jax.ShapeDtypeStruct((n, GDN_QK), F32),
            jax.ShapeDtypeStruct((n, 128), F32), jax.ShapeDtypeStruct((n, 3 * SC_DIM), F32),
            jax.ShapeDtypeStruct((n, SB_W), BF16)]


def _in_proj_t(x, g, w, wkv_t, kt_all, vt_all, layer, batch, tm):
    n, d = x.shape
    t_len = n // batch
    nt = t_len // tm
    row = lambda width: pl.BlockSpec((tm, width), lambda b, t: (b * nt + t, 0))
    stacked = pl.BlockSpec((None, None, SB_W, tm), lambda b, t: (layer, b, 0, t))
    per_layer = pl.BlockSpec((None, SB_W, tm), lambda b, t: (b, 0, t))
    any_spec = pl.BlockSpec(memory_space=pl.ANY)
    out_shape = _in_proj_outs(n) + [
        jax.ShapeDtypeStruct(kt_all.shape, F32), jax.ShapeDtypeStruct(vt_all.shape, F32),
        jax.ShapeDtypeStruct((batch, SB_W, t_len), BF16), jax.ShapeDtypeStruct((batch, SB_W, t_len), BF16)]
    return pl.pallas_call(
        functools.partial(_in_proj_kernel, transposed_kv=True),
        out_shape=tuple(out_shape),
        grid=(batch, nt),
        in_specs=[row(d), _const_spec((1, d)), _const_spec(w.shape), _const_spec(wkv_t.shape),
                  any_spec, any_spec],
        out_specs=(row(GDN_CONV_DIM), row(GDN_QK), row(128), row(3 * SC_DIM), row(SB_W),
                   stacked, stacked, per_layer, per_layer),
        input_output_aliases={4: 5, 5: 6},
        compiler_params=_params(2),
        name="in_proj",
    )(x, g, w, wkv_t, kt_all, vt_all)


def _in_proj(x, g, w, wkv_t, tm):
    n, d = x.shape
    row = lambda width: pl.BlockSpec((tm, width), lambda i: (i, 0))
    out_shape = _in_proj_outs(n) + [jax.ShapeDtypeStruct((n, SB_W), F32)] * 2
    return pl.pallas_call(
        functools.partial(_in_proj_kernel, transposed_kv=False),
        out_shape=tuple(out_shape),
        grid=(n // tm,),
        in_specs=[row(d), _const_spec((1, d)), _const_spec(w.shape), _const_spec(wkv_t.shape)],
        out_specs=(row(GDN_CONV_DIM), row(GDN_QK), row(128), row(3 * SC_DIM), row(SB_W),
                   row(SB_W), row(SB_W)),
        compiler_params=_params(1),
        name="in_proj",
    )(x, g, w, wkv_t)


def _split2(x):
    hi = x.astype(BF16)
    return hi, (x - hi.astype(F32)).astype(BF16)


def _dot2(a, b_exact):
    hi, lo = _split2(a)
    return _dot(hi, b_exact) + _dot(lo, b_exact)


def _dot2_left(a_exact, b):
    hi, lo = _split2(b)
    return _dot(a_exact, hi) + _dot(a_exact, lo)


def _gdn_kernel(qkv_ref, gz_ref, gab_ref, sbcx_ref, hg_ref, hs_ref, s0_ref,
                cw_ref, scw_ref, gp_ref, gng_ref, sng_ref,
                ya_ref, yb_ref, s_ref, tg_ref, ts_ref, carry_g, carry_s, *, tt, chunk):
    nh = GDN_HEADS
    sw = nh * chunk
    sec = max(sw, 128)
    n_chunks = tt // chunk

    @pl.when(pl.program_id(1) == 0)
    def _():
        carry_g[...] = hg_ref[...]
        carry_s[...] = hs_ref[...]
        s_ref[...] = s0_ref[...]

    x = qkv_ref[...]
    qn, kn, vv = [_gdn_front(x[:, GDN_QK * p:GDN_QK * (p + 1)], carry_g[:, GDN_QK * p:GDN_QK * (p + 1)],
                             cw_ref, p) for p in range(3)]
    carry_g[...] = x[tt - HIST:tt, :]
    tg_ref[...] = x[tt - HIST:tt, :]
    yb, pre_s = _short_conv(sbcx_ref[...], carry_s[...], scw_ref, sng_ref)
    yb_ref[...] = yb
    carry_s[...] = pre_s[tt - HIST:tt, :]
    ts_ref[...] = pre_s[tt - HIST:tt, :]

    slab = gab_ref[...]
    beta_s = _sigmoid(slab)
    g_s = -jnp.exp(gp_ref[0:1, :]) * _softplus(slab + gp_ref[1:2, :])
    ri = lax.broadcasted_iota(jnp.int32, (tt, tt), 0)
    ci = lax.broadcasted_iota(jnp.int32, (tt, tt), 1)
    tri = jnp.where(jnp.logical_and(ri // chunk == ci // chunk, ri >= ci), 1.0, 0.0).astype(BF16)
    gc = _dot2_left(tri, g_s)
    lane128 = lax.broadcasted_iota(jnp.int32, (1, 128), 1)
    slab2 = jnp.where(lane128 < nh, beta_s, gc)
    ew = 2 * GDN_QK + 2 * sec
    er = lax.broadcasted_iota(jnp.int32, (128, ew), 0)
    ec = lax.broadcasted_iota(jnp.int32, (128, ew), 1)
    src_row = jnp.where(ec < GDN_QK, ec // GDN_DK,
              jnp.where(ec < 2 * GDN_QK, nh + (ec - GDN_QK) // GDN_DK,
              jnp.where(ec < 2 * GDN_QK + sec, (ec - 2 * GDN_QK) // chunk,
                        nh + (ec - 2 * GDN_QK - sec) // chunk)))
    expand = jnp.where(er == src_row, 1.0, 0.0).astype(BF16)
    xp = _dot2(slab2, expand)
    b512 = xp[:, 0:GDN_QK]
    gc512 = xp[:, GDN_QK:2 * GDN_QK]
    b_s = xp[:, 2 * GDN_QK:2 * GDN_QK + sec][:, 0:sw]
    gc_s = xp[:, 2 * GDN_QK + sec:2 * GDN_QK + 2 * sec][:, 0:sw]
    eg512 = jnp.exp(gc512)

    qb = qn.astype(BF16)
    kb = kn.astype(BF16)
    qe = (qn * eg512).astype(BF16)
    rv = vv * b512
    rk = kn * (b512 * eg512)

    r_s = lax.broadcasted_iota(jnp.int32, (chunk, sw), 0)
    c_s = lax.broadcasted_iota(jnp.int32, (chunk, sw), 1)
    c_in = c_s - (c_s // chunk) * chunk
    causal_s = r_s >= c_in
    strict_s = r_s > c_in
    eye_s = jnp.where(r_s == c_in, 1.0, 0.0).astype(F32)
    pair_mask = jnp.logical_and(r_s // 2 == c_in // 2, strict_s)
    level_masks = []
    m = 2
    while m < chunk:
        level_masks.append(jnp.logical_and(
            jnp.logical_and(r_s // (2 * m) == c_in // (2 * m), r_s // m != c_in // m), strict_s))
        m *= 2
    bd_r = lax.broadcasted_iota(jnp.int32, (sw, sw), 0) // chunk
    bd_c = lax.broadcasted_iota(jnp.int32, (sw, sw), 1) // chunk
    bd_mask = bd_r == bd_c
    bdw_r = lax.broadcasted_iota(jnp.int32, (sw, GDN_QK), 0) // chunk
    bdw_c = lax.broadcasted_iota(jnp.int32, (sw, GDN_QK), 1) // GDN_DK
    bdw_mask = bdw_r == bdw_c
    zero_b = jnp.zeros((), BF16)

    def bd(a):
        return jnp.where(bd_mask, jnp.concatenate([a] * nh, axis=0), zero_b)

    def bdw(a):
        return jnp.where(bdw_mask, jnp.concatenate([a] * nh, axis=0), zero_b)

    tr = lax.broadcasted_iota(jnp.int32, (tt, sw), 0)
    tc = lax.broadcasted_iota(jnp.int32, (tt, sw), 1)
    eye_t = tr - (tr // chunk) * chunk == tc - (tc // chunk) * chunk
    same_chunk = jnp.where(ri // chunk == ci // chunk, 1.0, 0.0).astype(BF16)
    row_all = _dot2_left(same_chunk, jnp.where(eye_t, gc_s, 0.0))

    l_b, inv, qkd = [], [], []
    for c in range(n_chunks):
        rows = slice(c * chunk, (c + 1) * chunk)
        kq = _dot_nt(jnp.concatenate([kb[rows], qb[rows]], axis=0), bdw(kb[rows]))
        kk = kq[0:chunk]
        qk = kq[chunk:2 * chunk]
        decay = jnp.exp(jnp.where(causal_s, gc_s[rows] - row_all[rows], -jnp.inf))
        lm = jnp.where(strict_s, b_s[rows] * kk * decay, 0.0)
        l_b.append(lm.astype(BF16))
        inv.append(eye_s - jnp.where(pair_mask, lm, 0.0))
        qkd.append((qk * decay).astype(BF16))

    for mask in level_masks:
        for c in range(n_chunks):
            x_b = inv[c].astype(BF16)
            cx = _dot(jnp.where(mask, l_b[c], zero_b), bd(x_b))
            inv[c] = inv[c] - _dot(x_b, bd(cx.astype(BF16)))

    u_all, w_all = [], []
    for c in range(n_chunks):
        rows = slice(c * chunk, (c + 1) * chunk)
        x_b = inv[c].astype(BF16)
        u_all.append(_dot(x_b, bdw(rv[rows].astype(BF16))))
        w_all.append(_dot(x_b, bdw(rk[rows].astype(BF16))).astype(BF16))

    gng = gng_ref[...]
    for c in range(n_chunks):
        rows = slice(c * chunk, (c + 1) * chunk)
        g_last = gc512[(c + 1) * chunk - 1:(c + 1) * chunk, :]
        kd = (kn[rows] * jnp.exp(g_last - gc512[rows])).astype(BF16)
        eg_last = jnp.exp(g_last)
        s_old = [s_ref[h] for h in range(nh)]
        s_b = [s.astype(BF16) for s in s_old]
        wq = [_dot(jnp.concatenate([w_all[c][:, GDN_DK * h:GDN_DK * (h + 1)],
                                    qe[rows][:, GDN_DK * h:GDN_DK * (h + 1)]], axis=0), s_b[h])
              for h in range(nh)]
        v_new = u_all[c] - jnp.concatenate([wq[h][0:chunk] for h in range(nh)], axis=1)
        vb = v_new.astype(BF16)
        o_state = jnp.concatenate([wq[h][chunk:2 * chunk] for h in range(nh)], axis=1)
        o = o_state + _dot(qkd[c], bdw(vb))
        for h in range(nh):
            lanes = slice(GDN_DK * h, GDN_DK * (h + 1))
            s_ref[h] = s_old[h] * eg_last[:, lanes] + _dot_tn(kd[:, lanes], vb[:, lanes])
        z = gz_ref[c * chunk:(c + 1) * chunk, :]
        ya_ref[c * chunk:(c + 1) * chunk, :] = (jnp.concatenate(
            [_rms(o[:, GDN_DK * h:GDN_DK * (h + 1)], gng) for h in range(nh)], axis=1)
            * _silu(z)).astype(BF16)


def _gdn(qkv, gz, gab, sbcx, hist_g, hist_s, s0, cw, scw, gp, gng, sng, batch, tt, chunk):
    n = qkv.shape[0]
    nt = n // batch // tt
    row = lambda width: pl.BlockSpec((tt, width), lambda b, t: (b * nt + t, 0))
    per_b3 = lambda width: pl.BlockSpec((None, HIST, width), lambda b, t: (b, 0, 0))
    state = pl.BlockSpec((None, GDN_HEADS, GDN_DK, GDN_DK), lambda b, t: (b, 0, 0, 0))
    out_shape = (
        jax.ShapeDtypeStruct((n, GDN_QK), BF16),
        jax.ShapeDtypeStruct((n, SC_DIM), BF16),
        jax.ShapeDtypeStruct((batch, GDN_HEADS, GDN_DK, GDN_DK), F32),
        jax.ShapeDtypeStruct((batch, HIST, GDN_CONV_DIM), F32),
        jax.ShapeDtypeStruct((batch, HIST, SC_DIM), F32),
    )
    return pl.pallas_call(
        functools.partial(_gdn_kernel, tt=tt, chunk=chunk),
        out_shape=out_shape,
        grid=(batch, nt),
        in_specs=[row(GDN_CONV_DIM), row(GDN_QK), row(128), row(3 * SC_DIM),
                  per_b3(GDN_CONV_DIM), per_b3(SC_DIM), state,
                  _const_spec(cw.shape), _const_spec(scw.shape), _const_spec(gp.shape),
                  _const_spec(gng.shape), _const_spec(sng.shape)],
        out_specs=(row(GDN_QK), row(SC_DIM), state, per_b3(GDN_CONV_DIM), per_b3(SC_DIM)),
        scratch_shapes=[pltpu.VMEM((HIST, GDN_CONV_DIM), F32), pltpu.VMEM((HIST, SC_DIM), F32)],
        compiler_params=_params(2),
        name="gdn",
    )(qkv, gz, gab, sbcx, hist_g, hist_s, s0, cw, scw, gp, gng, sng)


def _sb_kernel(q_ref, kd_ref, vd_ref, kp_ref, vp_ref, g_ref, y_ref, acc_ref, c_ref,
               *, tq, tkd, tk, n_past_static):
    lane = lax.broadcasted_iota(jnp.int32, (1, SB_W), 1) // SB_DH
    rowh = lax.broadcasted_iota(jnp.int32, (SB_W, 1), 0) // SB_DH
    q = q_ref[...]
    zero_b = jnp.zeros((), BF16)
    q_heads = [jnp.where(lane == h, q, zero_b) for h in range(SB_HEADS)]

    acc_ref[...] = jnp.zeros_like(acc_ref)
    c_ref[...] = jnp.zeros_like(c_ref)

    def later_matrix(width):
        r = lax.broadcasted_iota(jnp.int32, (width, width), 0)
        c = lax.broadcasted_iota(jnp.int32, (width, width), 1)
        return jnp.where(r > c, 1.0, 0.0).astype(BF16)

    def block(kt, vt, later, mask):
        pv = jnp.zeros((tq, SB_W), F32)
        for h in range(SB_HEADS):
            z = _dot(q_heads[h], kt)
            sp = _softplus(z)
            l1 = -sp if mask is None else jnp.where(mask, -sp, 0.0)
            l1_hi, l1_lo = _split2(l1)
            rc = _dot(l1_hi, later) + _dot(l1_lo, later)
            carry = c_ref[:, h:h + 1]
            a = jnp.exp((z - sp) + rc + carry)
            if mask is not None:
                a = jnp.where(mask, a, 0.0)
            pv = pv + _dot_nt(a.astype(BF16), jnp.where(rowh == h, vt, zero_b))
            c_ref[:, h:h + 1] = carry + rc[:, 0:1] + l1[:, 0:1]
        acc_ref[...] += pv

    rq = lax.broadcasted_iota(jnp.int32, (tq, tkd), 0)
    ck = lax.broadcasted_iota(jnp.int32, (tq, tkd), 1)
    block(kd_ref[...], vd_ref[...], later_matrix(tkd), ck < rq)

    if n_past_static is None:
        n_past = pl.program_id(1) * (tq // tk)
    else:
        n_past = n_past_static
    later_past = later_matrix(tk)

    def cond(state):
        j, live = state
        return jnp.logical_and(j >= 0, live)

    def body(state):
        j, _ = state
        start = pl.multiple_of(j * tk, tk)
        kt = kp_ref[:, pl.ds(start, tk)].astype(BF16)
        vt = vp_ref[:, pl.ds(start, tk)].astype(BF16)
        block(kt, vt, later_past, None)
        live = jnp.max(c_ref[:, 0:SB_HEADS]) > SB_LOG_ZERO
        return j - 1, live

    lax.while_loop(cond, body, (n_past - 1, jnp.bool_(True)))

    o = acc_ref[...]
    ms = _dot(o * o, _group_mean_matrix(SB_W, SB_DH), HI)
    y_ref[...] = (o * lax.rsqrt(ms + EPS) * g_ref[...]).astype(BF16)


def _sb_attn(q, kd, vd, kp, vp, g, batch, tq, tkd, tk, n_past_static, layer=None):
    n = q.shape[0]
    nq = n // batch // tq
    qrow = pl.BlockSpec((tq, SB_W), lambda b, i: (b * nq + i, 0))
    if kd.shape[2] == tkd:
        diag = pl.BlockSpec((None, SB_W, tkd), lambda b, i: (b, 0, 0))
    else:
        diag = pl.BlockSpec((None, SB_W, tkd), lambda b, i: (b, 0, i))
    if layer is None:
        past = pl.BlockSpec((None, SB_W, kp.shape[2]), lambda b, i: (b, 0, 0))
    else:
        past = pl.BlockSpec((None, None, SB_W, kp.shape[3]), lambda b, i: (layer, b, 0, 0))
    return pl.pallas_call(
        functools.partial(_sb_kernel, tq=tq, tkd=tkd, tk=tk, n_past_static=n_past_static),
        out_shape=jax.ShapeDtypeStruct((n, SB_W), BF16),
        grid=(batch, nq),
        in_specs=[qrow, diag, diag, past, past, _const_spec(g.shape)],
        out_specs=qrow,
        scratch_shapes=[pltpu.VMEM((tq, SB_W), F32), pltpu.VMEM((tq, 128), F32)],
        compiler_params=_params(2),
        name="sb_attn",
    )(q, kd, vd, kp, vp, g)


def _mixmem_kernel(x_ref, ya_ref, yb_ref, yc_ref, wmix_ref, g_ref, wq_ref, qg_ref,
                   mk_ref, mv_ref, wo_ref, o_ref):
    x1 = (x_ref[...]
          + _dot(ya_ref[...], wmix_ref[0:GDN_QK, :])
          + _dot(yb_ref[...], wmix_ref[GDN_QK:GDN_QK + SC_DIM, :])
          + _dot(yc_ref[...], wmix_ref[GDN_QK + SC_DIM:GDN_QK + SC_DIM + SB_W, :]))
    h = _rms(x1, g_ref[...]).astype(BF16)
    qm = _dot(h, wq_ref[...])
    qg = qg_ref[...]
    outs = []
    for hd in range(MEM_HEADS):
        lanes = slice(MEM_DH * hd, MEM_DH * (hd + 1))
        qh = _rms(qm[:, lanes], qg).astype(BF16)
        kh = mk_ref[:, lanes].astype(BF16)
        vh = mv_ref[:, lanes].astype(BF16)
        s = _dot_nt(qh, kh) * (MEM_DH ** -0.5)
        e = jnp.exp(s - jnp.max(s, axis=-1, keepdims=True))
        p = e / jnp.sum(e, axis=-1, keepdims=True)
        outs.append(_dot(p.astype(BF16), vh).astype(BF16))
    om = jnp.concatenate(outs, axis=1)
    o_ref[...] = x1 + _dot(om, wo_ref[...])


def _mixmem(x, ya, yb, yc, wmix, g, wq, qg, mk, mv, wo, batch, tm):
    n, d = x.shape
    nt = n // batch // tm
    row = lambda width: pl.BlockSpec((tm, width), lambda b, t: (b * nt + t, 0))
    mem = pl.BlockSpec((None,) + mk.shape[1:], lambda b, t: (b, 0, 0))
    return pl.pallas_call(
        _mixmem_kernel,
        out_shape=jax.ShapeDtypeStruct((n, d), F32),
        grid=(batch, nt),
        in_specs=[row(d), row(GDN_QK), row(SC_DIM), row(SB_W),
                  _const_spec(wmix.shape), _const_spec(g.shape), _const_spec(wq.shape),
                  _const_spec(qg.shape), mem, mem, _const_spec(wo.shape)],
        out_specs=row(d),
        compiler_params=_params(2),
        name="mixmem",
    )(x, ya, yb, yc, wmix, g, wq, qg, mk, mv, wo)


def _ffn_kernel(x_ref, g_ref, wg_ref, wu_ref, cw_ref, wd_ref, hist_ref, o_ref, tail_ref, *, tm, d_ff):
    @pl.when(pl.program_id(1) == 0)
    def _():
        tail_ref[...] = hist_ref[...]

    x = x_ref[...]
    h = _rms(x, g_ref[...]).astype(BF16)
    acc = x
    for c0 in range(0, d_ff, FF_CHUNK):
        cols = slice(c0, c0 + FF_CHUNK)
        gate = _dot(h, wg_ref[:, cols])
        hist = tail_ref[:, cols]
        conv = (_shift_rows(gate, hist, 2) * cw_ref[0:1, cols]
                + _shift_rows(gate, hist, 1) * cw_ref[1:2, cols] + gate * cw_ref[2:3, cols])
        tail_ref[:, cols] = gate[tm - HIST:tm, :]
        up = _dot(h, wu_ref[:, cols])
        hid = (_silu(conv) * up).astype(BF16)
        acc = acc + _dot(hid, wd_ref[cols, :])
    o_ref[...] = acc


def _ffn(x, g, wg, wu, cw, wd, hist, batch, tm):
    n, d = x.shape
    d_ff = wg.shape[1]
    nt = n // batch // tm
    row = pl.BlockSpec((tm, d), lambda b, t: (b * nt + t, 0))
    per_b = pl.BlockSpec((None, HIST, d_ff), lambda b, t: (b, 0, 0))
    return pl.pallas_call(
        functools.partial(_ffn_kernel, tm=tm, d_ff=d_ff),
        out_shape=(jax.ShapeDtypeStruct((n, d), F32),
                   jax.ShapeDtypeStruct((batch, HIST, d_ff), F32)),
        grid=(batch, nt),
        in_specs=[row, _const_spec(g.shape), _const_spec(wg.shape), _const_spec(wu.shape),
                  _const_spec(cw.shape), _const_spec(wd.shape), per_b],
        out_specs=(row, per_b),
        compiler_params=_params(2),
        name="ffn",
    )(x, g, wg, wu, cw, wd, hist)


def _mem_kv_kernel(m_ref, g_ref, wk_ref, wv_ref, kg_ref, k_ref, v_ref):
    m = _rms(m_ref[...], g_ref[...]).astype(BF16)
    k = _dot(m, wk_ref[...])
    kg = kg_ref[...]
    for hd in range(MEM_HEADS):
        lanes = slice(MEM_DH * hd, MEM_DH * (hd + 1))
        k_ref[:, lanes] = _rms(k[:, lanes], kg)
    v_ref[...] = _dot(m, wv_ref[...])


def _mem_kv(mem, g, wk, wv, kg, tm):
    n, d = mem.shape
    w = wk.shape[1]
    row = lambda width: pl.BlockSpec((tm, width), lambda i: (i, 0))
    return pl.pallas_call(
        _mem_kv_kernel,
        out_shape=(jax.ShapeDtypeStruct((n, w), F32), jax.ShapeDtypeStruct((n, w), F32)),
        grid=(n // tm,),
        in_specs=[row(d), _const_spec(g.shape), _const_spec(wk.shape), _const_spec(wv.shape),
                  _const_spec(kg.shape)],
        out_specs=(row(w), row(w)),
        compiler_params=_params(1),
        name="mem_kv",
    )(mem, g, wk, wv, kg)


def _pad_hist(buf):
    return jnp.pad(buf, ((0, 0), (HIST - buf.shape[1], 0), (0, 0)))


def _largest_tile(total, cap):
    t = min(total, cap)
    while total % t:
        t //= 2
    return t


def _layer(x, batch, lw, mem_k, mem_v, s0, hist_g, hist_s, hist_f, layer,
           kv_stack=None, sb_cache=None):
    n = x.shape[0]
    t_len = n // batch
    chunk = min(CHUNK, t_len)
    tt = _largest_tile(t_len, 8 * chunk)
    tm = _largest_tile(t_len, 512)

    if kv_stack is not None:
        qkv, gz, gab, sbcx, cq, kt_all, vt_all, ktb, vtb = _in_proj_t(
            x, lw["norm_mix_g"], lw["w_in"], lw["w_kv_t"], kv_stack[0], kv_stack[1], layer, batch, tm)
        kv_out = (kt_all, vt_all)
    else:
        qkv, gz, gab, sbcx, cq, ck, cv = _in_proj(
            x, lw["norm_mix_g"], lw["w_in"], lw["w_kv_t"], _largest_tile(n, 512))
        kv_out = (ck, cv)

    ya, yb, s_new, tail_g, tail_s = _gdn(
        qkv, gz, gab, sbcx, hist_g, hist_s, s0, lw["gdn_conv_w"], lw["sc_conv_w"],
        lw["gate_params"], lw["gdn_norm_g"], lw["sc_norm_g"], batch, tt, chunk)

    if kv_stack is not None:
        tq = _largest_tile(t_len, 256)
        yc = _sb_attn(cq, ktb, vtb, ktb, vtb, lw["sb_norm_g"], batch, tq, tq, tq, None)
    else:
        tkd = 128
        new_t = lambda a: jnp.pad(jnp.swapaxes(a.reshape(batch, t_len, SB_W), 1, 2).astype(BF16),
                                  ((0, 0), (0, 0), (0, tkd - t_len)))
        p_len = sb_cache[0].shape[3]
        tk = _largest_tile(p_len, 256)
        yc = _sb_attn(cq, new_t(ck), new_t(cv), sb_cache[0], sb_cache[1], lw["sb_norm_g"],
                      batch, t_len, tkd, tk, p_len // tk, layer=layer)

    x = _mixmem(x, ya, yb, yc, lw["w_mix_out"], lw["norm_mem_g"], lw["w_mq"], lw["mq_norm_g"],
                mem_k, mem_v, lw["w_mo"], batch, tm)
    x, tail_f = _ffn(x, lw["norm_ffn_g"], lw["w_gate"], lw["w_up"], lw["ffn_conv_w"],
                     lw["w_down"], hist_f, batch, tm)
    return x, s_new, tail_g[:, HIST - 3:], tail_s[:, HIST - 2:], kv_out, tail_f[:, HIST - 2:]


def kernel(x_prompt, x_sample, mem_prompt, state_gdn, cache_gdn_conv, cache_sc_conv, cache_sb_k, cache_sb_v, cache_mem_k, cache_mem_v, cache_ffn_conv, norm_mix_g, w_in, gdn_conv_w, gdn_A_log, gdn_dt_bias, gdn_norm_g, sc_conv_w, sc_norm_g, sb_norm_g, w_mix_out, norm_mem_g, mem_in_norm_g, w_mq, w_mk, w_mv, mq_norm_g, mk_norm_g, w_mo, norm_ffn_g, w_gate, w_up, ffn_conv_w, w_down):
    depth = w_in.shape[0]
    bp, tp, d = x_prompt.shape
    bs, ts, _ = x_sample.shape
    n_mem = mem_prompt.shape[1]
    d_ff = w_gate.shape[2]
    p_len = cache_sb_k.shape[2]
    n_gate = 2 * GDN_QK + 2 * GDN_QK
    n_kv = w_in.shape[2] - 2 * SB_W

    xp = x_prompt.reshape(bp * tp, d)
    xs = x_sample.reshape(bs * ts, d)
    mem = mem_prompt.reshape(bp * n_mem, d)
    zeros = lambda b, c: jnp.zeros((b, HIST, c), F32)
    to_t = lambda c: jnp.transpose(c, (0, 1, 3, 4, 2)).reshape(depth, bs, SB_W, p_len)
    sb_cache = (to_t(cache_sb_k), to_t(cache_sb_v))
    kv_stack = (jnp.zeros((depth, bp, SB_W, tp), F32), jnp.zeros((depth, bp, SB_W, tp), F32))

    outs = [[] for _ in range(14)]
    for l in range(depth):
        w = w_in[l]
        w_packed = jnp.concatenate(
            [w[:, :n_gate + 2 * GDN_HEADS],
             jnp.zeros((d, 128 - 2 * GDN_HEADS), w.dtype),
             w[:, n_gate + 2 * GDN_HEADS:n_kv]], axis=1).astype(BF16)
        gate_params = jnp.zeros((8, 128), F32)
        gate_params = gate_params.at[0, GDN_HEADS:2 * GDN_HEADS].set(gdn_A_log[l])
        gate_params = gate_params.at[1, GDN_HEADS:2 * GDN_HEADS].set(gdn_dt_bias[l])
        row = lambda a: a[l].reshape(1, -1)
        lw = dict(
            norm_mix_g=row(norm_mix_g), w_in=w_packed, w_kv_t=w[:, n_kv:].T.astype(BF16),
            gdn_conv_w=gdn_conv_w[l],
            gate_params=gate_params, gdn_norm_g=row(gdn_norm_g), sc_conv_w=sc_conv_w[l],
            sc_norm_g=row(sc_norm_g), sb_norm_g=row(sb_norm_g),
            w_mix_out=w_mix_out[l].astype(BF16), norm_mem_g=row(norm_mem_g),
            w_mq=w_mq[l].astype(BF16), mq_norm_g=row(mq_norm_g), w_mo=w_mo[l].astype(BF16),
            norm_ffn_g=row(norm_ffn_g), w_gate=w_gate[l].astype(BF16), w_up=w_up[l].astype(BF16),
            ffn_conv_w=ffn_conv_w[l], w_down=w_down[l].astype(BF16))

        mk, mv = _mem_kv(mem, row(mem_in_norm_g), w_mk[l].astype(BF16), w_mv[l].astype(BF16),
                         row(mk_norm_g), _largest_tile(bp * n_mem, 512))
        mk3 = mk.reshape(bp, n_mem, MEM_HEADS * MEM_DH)
        mv3 = mv.reshape(bp, n_mem, MEM_HEADS * MEM_DH)
        xp, s_n, gc_n, sc_n, kv_stack, fc_n = _layer(
            xp, bp, lw, mk3, mv3, jnp.zeros((bp, GDN_HEADS, GDN_DK, GDN_DK), F32),
            zeros(bp, GDN_CONV_DIM), zeros(bp, SC_DIM), zeros(bp, d_ff), l, kv_stack=kv_stack)
        for i, a in ((0, s_n), (1, gc_n), (2, sc_n), (5, mk.reshape(bp, n_mem, MEM_HEADS, MEM_DH)),
                     (6, mv.reshape(bp, n_mem, MEM_HEADS, MEM_DH)), (7, fc_n)):
            outs[i].append(a)

        xs, s_n, gc_n, sc_n, (k_n, v_n), fc_n = _layer(
            xs, bs, lw,
            cache_mem_k[l].reshape(bs, n_mem, MEM_HEADS * MEM_DH),
            cache_mem_v[l].reshape(bs, n_mem, MEM_HEADS * MEM_DH),
            state_gdn[l], _pad_hist(cache_gdn_conv[l]), _pad_hist(cache_sc_conv[l]),
            _pad_hist(cache_ffn_conv[l]), l, sb_cache=sb_cache)
        for i, a in enumerate((s_n, gc_n, sc_n, k_n.reshape(bs, ts, SB_HEADS, SB_DH),
                               v_n.reshape(bs, ts, SB_HEADS, SB_DH), fc_n)):
            outs[8 + i].append(a)

    from_t = lambda a: jnp.transpose(a.reshape(depth, bp, SB_HEADS, SB_DH, tp), (0, 1, 4, 2, 3))
    stacked = [jnp.stack(o) if o else None for o in outs]
    stacked[3], stacked[4] = from_t(kv_stack[0]), from_t(kv_stack[1])
    return (xp.reshape(bp, tp, d), xs.reshape(bs, ts, d)) + tuple(stacked)
```

```python
import functools

import jax
import jax.numpy as jnp
from jax import lax
from jax.experimental import pallas as pl
from jax.experimental.pallas import tpu as pltpu

F32 = jnp.float32
BF16 = jnp.bfloat16
HI = lax.Precision.HIGHEST
EPS = 1e-6

GDN_HEADS = 4
GDN_DK = 128
GDN_QK = 512
GDN_CONV_DIM = 1536
SC_DIM = 256
SB_HEADS = 4
SB_DH = 64
SB_W = 256
MEM_HEADS = 4
MEM_DH = 128
CHUNK = 64
HIST = 8
FF_CHUNK = 1024
SB_LOG_ZERO = -104.0
VMEM_LIMIT = 56 * 1024 * 1024

C_QKV, C_GZ, C_GAB, C_SBCX, C_CQ, C_CK = 0, 1536, 2048, 2176, 2944, 3200


def _dot(a, b, precision=None):
    return jnp.dot(a, b, preferred_element_type=F32, precision=precision)


def _dot_nt(a, b, precision=None):
    return lax.dot_general(a, b, (((1,), (1,)), ((), ())),
                           preferred_element_type=F32, precision=precision)


def _dot_tn(a, b, precision=None):
    return lax.dot_general(a, b, (((0,), (0,)), ((), ())),
                           preferred_element_type=F32, precision=precision)


def _rms(x, g):
    return x * lax.rsqrt(jnp.mean(x * x, axis=-1, keepdims=True) + EPS) * g


def _sigmoid(x):
    return 1.0 / (1.0 + jnp.exp(-x))


def _silu(x):
    return x * _sigmoid(x)


def _softplus(x):
    return jnp.maximum(x, 0.0) + jnp.log(1.0 + jnp.exp(-jnp.abs(x)))


def _shift_rows(x, hist, d):
    n_seq = hist.shape[0]
    seg = x.shape[0] // n_seq
    r = pltpu.roll(x, d, 0)
    row = lax.broadcasted_iota(jnp.int32, (HIST, x.shape[1]), 0)
    pieces = []
    for j in range(n_seq):
        pieces.append(jnp.where(row < d, pltpu.roll(hist[j], d, 0), r[j * seg:j * seg + HIST]))
        pieces.append(r[j * seg + HIST:(j + 1) * seg])
    return jnp.concatenate(pieces, axis=0)


def _last_rows(x, n_seq):
    seg = x.shape[0] // n_seq
    return jnp.stack([x[(j + 1) * seg - HIST:(j + 1) * seg] for j in range(n_seq)], axis=0)


def _group_mean_matrix(width, group):
    r = lax.broadcasted_iota(jnp.int32, (width, width), 0) // group
    c = lax.broadcasted_iota(jnp.int32, (width, width), 1) // group
    return jnp.where(r == c, 1.0 / group, 0.0).astype(F32)


def _params(n_axes):
    return pltpu.CompilerParams(dimension_semantics=("arbitrary",) * n_axes,
                                vmem_limit_bytes=VMEM_LIMIT)


def _const_spec(shape):
    nd = len(shape)
    return pl.BlockSpec(shape, lambda *_: (0,) * nd, pipeline_mode=pl.Buffered(1))


def _l2norm_heads(a):
    parts = []
    for h in range(GDN_HEADS):
        ah = a[:, GDN_DK * h:GDN_DK * (h + 1)]
        parts.append(ah * lax.rsqrt(jnp.sum(ah * ah, axis=-1, keepdims=True) + EPS))
    return jnp.concatenate(parts, axis=1)


def _gdn_front(x, hist, cw_ref, part):
    cols = slice(GDN_QK * part, GDN_QK * (part + 1))
    conv = (_shift_rows(x, hist, 3) * cw_ref[0:1, cols] + _shift_rows(x, hist, 2) * cw_ref[1:2, cols]
            + _shift_rows(x, hist, 1) * cw_ref[2:3, cols] + x * cw_ref[3:4, cols])
    act = _silu(conv)
    if part == 0:
        return _l2norm_heads(act) * (GDN_DK ** -0.5)
    return _l2norm_heads(act) if part == 1 else act


def _short_conv(sbcx, hist, scw_ref, sng_ref):
    s_b = sbcx[:, 0:SC_DIM]
    pre = sbcx[:, SC_DIM:2 * SC_DIM] * sbcx[:, 2 * SC_DIM:3 * SC_DIM]
    u_c = (_shift_rows(pre, hist, 2) * scw_ref[0:1, :] + _shift_rows(pre, hist, 1) * scw_ref[1:2, :]
           + pre * scw_ref[2:3, :])
    yb = s_b * u_c
    ms = _dot(yb * yb, _group_mean_matrix(SC_DIM, SC_DIM // 4), HI)
    return (yb * lax.rsqrt(ms + EPS) * sng_ref[...]).astype(BF16), pre


def _in_proj_kernel(*refs, transposed_kv):
    if transposed_kv:
        (x_ref, g_ref, w_ref, wkv_ref, _, _, qkv_ref, gz_ref, gab_ref, sbcx_ref, cq_ref,
         kt_ref, vt_ref, ktb_ref, vtb_ref) = refs
    else:
        (x_ref, g_ref, w_ref, wkv_ref, qkv_ref, gz_ref, gab_ref, sbcx_ref, cq_ref,
         ck_ref, cv_ref) = refs
    h = _rms(x_ref[...], g_ref[...]).astype(BF16)

    def proj(c0, c1):
        return _dot(h, w_ref[:, c0:c1])

    for c in range(0, GDN_CONV_DIM, 512):
        qkv_ref[:, c:c + 512] = proj(C_QKV + c, C_QKV + c + 512)
    gz_ref[...] = proj(C_GZ, C_GAB)
    gab_ref[...] = proj(C_GAB, C_SBCX)
    for c in range(0, 3 * SC_DIM, 256):
        sbcx_ref[:, c:c + 256] = proj(C_SBCX + c, C_SBCX + c + 256)
    cq_ref[...] = (proj(C_CQ, C_CK) * (SB_DH ** -0.5)).astype(BF16)
    if transposed_kv:
        kt = _dot_nt(wkv_ref[0:SB_W, :], h)
        kt_ref[...] = kt
        ktb_ref[...] = kt.astype(BF16)
        vt = _dot_nt(wkv_ref[SB_W:2 * SB_W, :], h)
        vt_ref[...] = vt
        vtb_ref[...] = vt.astype(BF16)
    else:
        ck_ref[...] = _dot_nt(h, wkv_ref[0:SB_W, :])
        cv_ref[...] = _dot_nt(h, wkv_ref[SB_W:2 * SB_W, :])


def _in_proj_outs(n):
    return [jax.ShapeDtypeStruct((n, GDN_CONV_DIM), F32), jax.ShapeDtypeStruct((n, GDN_QK), F32),
            jax.ShapeDtypeStruct((n, 128), F32), jax.ShapeDtypeStruct((n, 3 * SC_DIM), F32),
            jax.ShapeDtypeStruct((n, SB_W), BF16)]


def _in_proj_t(x, g, w, wkv_t, kt_all, vt_all, layer, batch, tm):
    n, d = x.shape
    t_len = n // batch
    nt = t_len // tm
    row = lambda width: pl.BlockSpec((tm, width), lambda b, t: (b * nt + t, 0))
    stacked = pl.BlockSpec((None, None, SB_W, tm), lambda b, t: (layer, b, 0, t))
    per_layer = pl.BlockSpec((None, SB_W, tm), lambda b, t: (b, 0, t))
    any_spec = pl.BlockSpec(memory_space=pl.ANY)
    out_shape = _in_proj_outs(n) + [
        jax.ShapeDtypeStruct(kt_all.shape, F32), jax.ShapeDtypeStruct(vt_all.shape, F32),
        jax.ShapeDtypeStruct((batch, SB_W, t_len), BF16), jax.ShapeDtypeStruct((batch, SB_W, t_len), BF16)]
    return pl.pallas_call(
        functools.partial(_in_proj_kernel, transposed_kv=True),
        out_shape=tuple(out_shape),
        grid=(batch, nt),
        in_specs=[row(d), _const_spec((1, d)), _const_spec(w.shape), _const_spec(wkv_t.shape),
                  any_spec, any_spec],
        out_specs=(row(GDN_CONV_DIM), row(GDN_QK), row(128), row(3 * SC_DIM), row(SB_W),
                   stacked, stacked, per_layer, per_layer),
        input_output_aliases={4: 5, 5: 6},
        compiler_params=_params(2),
        name="in_proj",
    )(x, g, w, wkv_t, kt_all, vt_all)


def _in_proj(x, g, w, wkv_t, tm):
    n, d = x.shape
    row = lambda width: pl.BlockSpec((tm, width), lambda i: (i, 0))
    out_shape = _in_proj_outs(n) + [jax.ShapeDtypeStruct((n, SB_W), F32)] * 2
    return pl.pallas_call(
        functools.partial(_in_proj_kernel, transposed_kv=False),
        out_shape=tuple(out_shape),
        grid=(n // tm,),
        in_specs=[row(d), _const_spec((1, d)), _const_spec(w.shape), _const_spec(wkv_t.shape)],
        out_specs=(row(GDN_CONV_DIM), row(GDN_QK), row(128), row(3 * SC_DIM), row(SB_W),
                   row(SB_W), row(SB_W)),
        compiler_params=_params(1),
        name="in_proj",
    )(x, g, w, wkv_t)


def _split2(x):
    hi = x.astype(BF16)
    return hi, (x - hi.astype(F32)).astype(BF16)


def _dot2(a, b_exact):
    hi, lo = _split2(a)
    return _dot(hi, b_exact) + _dot(lo, b_exact)


def _dot2_left(a_exact, b):
    hi, lo = _split2(b)
    return _dot(a_exact, hi) + _dot(a_exact, lo)


def _gdn_kernel(qkv_ref, gz_ref, gab_ref, sbcx_ref, hg_ref, hs_ref, s0_ref,
                cw_ref, scw_ref, gp_ref, gng_ref, sng_ref,
                ya_ref, yb_ref, s_ref, tg_ref, ts_ref, carry_g, carry_s, *, tt, chunk):
    nh = GDN_HEADS
    sw = nh * chunk
    sec = max(sw, 128)
    n_chunks = tt // chunk

    @pl.when(pl.program_id(1) == 0)
    def _():
        carry_g[...] = hg_ref[...]
        carry_s[...] = hs_ref[...]
        s_ref[...] = s0_ref[...]

    x = qkv_ref[...]
    qn, kn, vv = [_gdn_front(x[:, GDN_QK * p:GDN_QK * (p + 1)], carry_g[:, :, GDN_QK * p:GDN_QK * (p + 1)],
                             cw_ref, p) for p in range(3)]
    carry_g[...] = _last_rows(x, 1)
    tg_ref[...] = _last_rows(x, 1)
    yb, pre_s = _short_conv(sbcx_ref[...], carry_s[...], scw_ref, sng_ref)
    yb_ref[...] = yb
    carry_s[...] = _last_rows(pre_s, 1)
    ts_ref[...] = _last_rows(pre_s, 1)

    slab = gab_ref[...]
    beta_s = _sigmoid(slab)
    g_s = -jnp.exp(gp_ref[0:1, :]) * _softplus(slab + gp_ref[1:2, :])
    ri = lax.broadcasted_iota(jnp.int32, (tt, tt), 0)
    ci = lax.broadcasted_iota(jnp.int32, (tt, tt), 1)
    tri = jnp.where(jnp.logical_and(ri // chunk == ci // chunk, ri >= ci), 1.0, 0.0).astype(BF16)
    gc = _dot2_left(tri, g_s)
    lane128 = lax.broadcasted_iota(jnp.int32, (1, 128), 1)
    slab2 = jnp.where(lane128 < nh, beta_s, gc)
    ew = 2 * GDN_QK + 2 * sec
    er = lax.broadcasted_iota(jnp.int32, (128, ew), 0)
    ec = lax.broadcasted_iota(jnp.int32, (128, ew), 1)
    src_row = jnp.where(ec < GDN_QK, ec // GDN_DK,
              jnp.where(ec < 2 * GDN_QK, nh + (ec - GDN_QK) // GDN_DK,
              jnp.where(ec < 2 * GDN_QK + sec, (ec - 2 * GDN_QK) // chunk,
                        nh + (ec - 2 * GDN_QK - sec) // chunk)))
    expand = jnp.where(er == src_row, 1.0, 0.0).astype(BF16)
    xp = _dot2(slab2, expand)
    b512 = xp[:, 0:GDN_QK]
    gc512 = xp[:, GDN_QK:2 * GDN_QK]
    b_s = xp[:, 2 * GDN_QK:2 * GDN_QK + sec][:, 0:sw]
    gc_s = xp[:, 2 * GDN_QK + sec:2 * GDN_QK + 2 * sec][:, 0:sw]
    eg512 = jnp.exp(gc512)

    qb = qn.astype(BF16)
    kb = kn.astype(BF16)
    qe = (qn * eg512).astype(BF16)
    rv = vv * b512
    rk = kn * (b512 * eg512)

    r_s = lax.broadcasted_iota(jnp.int32, (chunk, sw), 0)
    c_s = lax.broadcasted_iota(jnp.int32, (chunk, sw), 1)
    c_in = c_s - (c_s // chunk) * chunk
    causal_s = r_s >= c_in
    strict_s = r_s > c_in
    eye_s = jnp.where(r_s == c_in, 1.0, 0.0).astype(F32)
    pair_mask = jnp.logical_and(r_s // 2 == c_in // 2, strict_s)
    level_masks = []
    m = 2
    while m < chunk:
        level_masks.append(jnp.logical_and(
            jnp.logical_and(r_s // (2 * m) == c_in // (2 * m), r_s // m != c_in // m), strict_s))
        m *= 2
    bd_r = lax.broadcasted_iota(jnp.int32, (sw, sw), 0) // chunk
    bd_c = lax.broadcasted_iota(jnp.int32, (sw, sw), 1) // chunk
    bd_mask = bd_r == bd_c
    bdw_r = lax.broadcasted_iota(jnp.int32, (sw, GDN_QK), 0) // chunk
    bdw_c = lax.broadcasted_iota(jnp.int32, (sw, GDN_QK), 1) // GDN_DK
    bdw_mask = bdw_r == bdw_c
    zero_b = jnp.zeros((), BF16)

    def bd(a):
        return jnp.where(bd_mask, jnp.concatenate([a] * nh, axis=0), zero_b)

    def bdw(a):
        return jnp.where(bdw_mask, jnp.concatenate([a] * nh, axis=0), zero_b)

    tr = lax.broadcasted_iota(jnp.int32, (tt, sw), 0)
    tc = lax.broadcasted_iota(jnp.int32, (tt, sw), 1)
    eye_t = tr - (tr // chunk) * chunk == tc - (tc // chunk) * chunk
    same_chunk = jnp.where(ri // chunk == ci // chunk, 1.0, 0.0).astype(BF16)
    row_all = _dot2_left(same_chunk, jnp.where(eye_t, gc_s, 0.0))

    l_b, inv, qkd = [], [], []
    for c in range(n_chunks):
        rows = slice(c * chunk, (c + 1) * chunk)
        kq = _dot_nt(jnp.concatenate([kb[rows], qb[rows]], axis=0), bdw(kb[rows]))
        kk = kq[0:chunk]
        qk = kq[chunk:2 * chunk]
        decay = jnp.exp(jnp.where(causal_s, gc_s[rows] - row_all[rows], -jnp.inf))
        lm = jnp.where(strict_s, b_s[rows] * kk * decay, 0.0)
        l_b.append(lm.astype(BF16))
        inv.append(eye_s - jnp.where(pair_mask, lm, 0.0))
        qkd.append((qk * decay).astype(BF16))

    for mask in level_masks:
        for c in range(n_chunks):
            x_b = inv[c].astype(BF16)
            cx = _dot(jnp.where(mask, l_b[c], zero_b), bd(x_b))
            inv[c] = inv[c] - _dot(x_b, bd(cx.astype(BF16)))

    u_all, w_all = [], []
    for c in range(n_chunks):
        rows = slice(c * chunk, (c + 1) * chunk)
        x_b = inv[c].astype(BF16)
        u_all.append(_dot(x_b, bdw(rv[rows].astype(BF16))))
        w_all.append(_dot(x_b, bdw(rk[rows].astype(BF16))).astype(BF16))

    gng = gng_ref[...]
    for c in range(n_chunks):
        rows = slice(c * chunk, (c + 1) * chunk)
        g_last = gc512[(c + 1) * chunk - 1:(c + 1) * chunk, :]
        kd = (kn[rows] * jnp.exp(g_last - gc512[rows])).astype(BF16)
        eg_last = jnp.exp(g_last)
        s_old = [s_ref[h] for h in range(nh)]
        s_b = [s.astype(BF16) for s in s_old]
        wq = [_dot(jnp.concatenate([w_all[c][:, GDN_DK * h:GDN_DK * (h + 1)],
                                    qe[rows][:, GDN_DK * h:GDN_DK * (h + 1)]], axis=0), s_b[h])
              for h in range(nh)]
        v_new = u_all[c] - jnp.concatenate([wq[h][0:chunk] for h in range(nh)], axis=1)
        vb = v_new.astype(BF16)
        o_state = jnp.concatenate([wq[h][chunk:2 * chunk] for h in range(nh)], axis=1)
        o = o_state + _dot(qkd[c], bdw(vb))
        for h in range(nh):
            lanes = slice(GDN_DK * h, GDN_DK * (h + 1))
            s_ref[h] = s_old[h] * eg_last[:, lanes] + _dot_tn(kd[:, lanes], vb[:, lanes])
        z = gz_ref[c * chunk:(c + 1) * chunk, :]
        ya_ref[c * chunk:(c + 1) * chunk, :] = (jnp.concatenate(
            [_rms(o[:, GDN_DK * h:GDN_DK * (h + 1)], gng) for h in range(nh)], axis=1)
            * _silu(z)).astype(BF16)


def _gdn(qkv, gz, gab, sbcx, hist_g, hist_s, s0, cw, scw, gp, gng, sng, batch, tt, chunk):
    n = qkv.shape[0]
    nt = n // batch // tt
    row = lambda width: pl.BlockSpec((tt, width), lambda b, t: (b * nt + t, 0))
    per_b3 = lambda width: pl.BlockSpec((1, HIST, width), lambda b, t: (b, 0, 0))
    state = pl.BlockSpec((None, GDN_HEADS, GDN_DK, GDN_DK), lambda b, t: (b, 0, 0, 0))
    out_shape = (
        jax.ShapeDtypeStruct((n, GDN_QK), BF16),
        jax.ShapeDtypeStruct((n, SC_DIM), BF16),
        jax.ShapeDtypeStruct((batch, GDN_HEADS, GDN_DK, GDN_DK), F32),
        jax.ShapeDtypeStruct((batch, HIST, GDN_CONV_DIM), F32),
        jax.ShapeDtypeStruct((batch, HIST, SC_DIM), F32),
    )
    return pl.pallas_call(
        functools.partial(_gdn_kernel, tt=tt, chunk=chunk),
        out_shape=out_shape,
        grid=(batch, nt),
        in_specs=[row(GDN_CONV_DIM), row(GDN_QK), row(128), row(3 * SC_DIM),
                  per_b3(GDN_CONV_DIM), per_b3(SC_DIM), state,
                  _const_spec(cw.shape), _const_spec(scw.shape), _const_spec(gp.shape),
                  _const_spec(gng.shape), _const_spec(sng.shape)],
        out_specs=(row(GDN_QK), row(SC_DIM), state, per_b3(GDN_CONV_DIM), per_b3(SC_DIM)),
        scratch_shapes=[pltpu.VMEM((1, HIST, GDN_CONV_DIM), F32), pltpu.VMEM((1, HIST, SC_DIM), F32)],
        compiler_params=_params(2),
        name="gdn",
    )(qkv, gz, gab, sbcx, hist_g, hist_s, s0, cw, scw, gp, gng, sng)


def _sb_kernel(q_ref, kd_ref, vd_ref, kp_ref, vp_ref, g_ref, y_ref, acc_ref, c_ref,
               *, tq, tkd, tk, n_past_static):
    lane = lax.broadcasted_iota(jnp.int32, (1, SB_W), 1) // SB_DH
    rowh = lax.broadcasted_iota(jnp.int32, (SB_W, 1), 0) // SB_DH
    q = q_ref[...]
    zero_b = jnp.zeros((), BF16)
    q_heads = [jnp.where(lane == h, q, zero_b) for h in range(SB_HEADS)]

    acc_ref[...] = jnp.zeros_like(acc_ref)
    c_ref[...] = jnp.zeros_like(c_ref)

    def later_matrix(width):
        r = lax.broadcasted_iota(jnp.int32, (width, width), 0)
        c = lax.broadcasted_iota(jnp.int32, (width, width), 1)
        return jnp.where(r > c, 1.0, 0.0).astype(BF16)

    def block(kt, vt, later, mask):
        pv = jnp.zeros((tq, SB_W), F32)
        for h in range(SB_HEADS):
            z = _dot(q_heads[h], kt)
            sp = _softplus(z)
            l1 = -sp if mask is None else jnp.where(mask, -sp, 0.0)
            l1_hi, l1_lo = _split2(l1)
            rc = _dot(l1_hi, later) + _dot(l1_lo, later)
            carry = c_ref[:, h:h + 1]
            a = jnp.exp((z - sp) + rc + carry)
            if mask is not None:
                a = jnp.where(mask, a, 0.0)
            pv = pv + _dot_nt(a.astype(BF16), jnp.where(rowh == h, vt, zero_b))
            c_ref[:, h:h + 1] = carry + rc[:, 0:1] + l1[:, 0:1]
        acc_ref[...] += pv

    rq = lax.broadcasted_iota(jnp.int32, (tq, tkd), 0)
    ck = lax.broadcasted_iota(jnp.int32, (tq, tkd), 1)
    block(kd_ref[...], vd_ref[...], later_matrix(tkd), ck < rq)

    if n_past_static is None:
        n_past = pl.program_id(1) * (tq // tk)
    else:
        n_past = n_past_static
    later_past = later_matrix(tk)

    def cond(state):
        j, live = state
        return jnp.logical_and(j >= 0, live)

    def body(state):
        j, _ = state
        start = pl.multiple_of(j * tk, tk)
        kt = kp_ref[:, pl.ds(start, tk)].astype(BF16)
        vt = vp_ref[:, pl.ds(start, tk)].astype(BF16)
        block(kt, vt, later_past, None)
        live = jnp.max(c_ref[:, 0:SB_HEADS]) > SB_LOG_ZERO
        return j - 1, live

    lax.while_loop(cond, body, (n_past - 1, jnp.bool_(True)))

    o = acc_ref[...]
    ms = _dot(o * o, _group_mean_matrix(SB_W, SB_DH), HI)
    y_ref[...] = (o * lax.rsqrt(ms + EPS) * g_ref[...]).astype(BF16)


def _sb_attn(q, kd, vd, kp, vp, g, batch, tq, tkd, tk, n_past_static, layer=None):
    n = q.shape[0]
    nq = n // batch // tq
    qrow = pl.BlockSpec((tq, SB_W), lambda b, i: (b * nq + i, 0))
    if kd.shape[2] == tkd:
        diag = pl.BlockSpec((None, SB_W, tkd), lambda b, i: (b, 0, 0))
    else:
        diag = pl.BlockSpec((None, SB_W, tkd), lambda b, i: (b, 0, i))
    if layer is None:
        past = pl.BlockSpec((None, SB_W, kp.shape[2]), lambda b, i: (b, 0, 0))
    else:
        past = pl.BlockSpec((None, None, SB_W, kp.shape[3]), lambda b, i: (layer, b, 0, 0))
    return pl.pallas_call(
        functools.partial(_sb_kernel, tq=tq, tkd=tkd, tk=tk, n_past_static=n_past_static),
        out_shape=jax.ShapeDtypeStruct((n, SB_W), BF16),
        grid=(batch, nq),
        in_specs=[qrow, diag, diag, past, past, _const_spec(g.shape)],
        out_specs=qrow,
        scratch_shapes=[pltpu.VMEM((tq, SB_W), F32), pltpu.VMEM((tq, 128), F32)],
        compiler_params=_params(2),
        name="sb_attn",
    )(q, kd, vd, kp, vp, g)


def _mixmem_kernel(x_ref, ya_ref, yb_ref, yc_ref, wmix_ref, g_ref, wq_ref, qg_ref,
                   mk_ref, mv_ref, wo_ref, o_ref, *, n_seq):
    x1 = (x_ref[...]
          + _dot(ya_ref[...], wmix_ref[0:GDN_QK, :])
          + _dot(yb_ref[...], wmix_ref[GDN_QK:GDN_QK + SC_DIM, :])
          + _dot(yc_ref[...], wmix_ref[GDN_QK + SC_DIM:GDN_QK + SC_DIM + SB_W, :]))
    h = _rms(x1, g_ref[...]).astype(BF16)
    qm = _dot(h, wq_ref[...])
    qg = qg_ref[...]
    seg = qm.shape[0] // n_seq
    rows_out = []
    for j in range(n_seq):
        outs = []
        for hd in range(MEM_HEADS):
            lanes = slice(MEM_DH * hd, MEM_DH * (hd + 1))
            qh = _rms(qm[j * seg:(j + 1) * seg, lanes], qg).astype(BF16)
            kh = mk_ref[j, :, lanes].astype(BF16)
            vh = mv_ref[j, :, lanes].astype(BF16)
            s = _dot_nt(qh, kh) * (MEM_DH ** -0.5)
            e = jnp.exp(s - jnp.max(s, axis=-1, keepdims=True))
            p = e / jnp.sum(e, axis=-1, keepdims=True)
            outs.append(_dot(p.astype(BF16), vh).astype(BF16))
        rows_out.append(jnp.concatenate(outs, axis=1))
    om = jnp.concatenate(rows_out, axis=0)
    o_ref[...] = x1 + _dot(om, wo_ref[...])


def _mixmem(x, ya, yb, yc, wmix, g, wq, qg, mk, mv, wo, batch, tm, n_seq):
    n, d = x.shape
    nt = n // (batch // n_seq) // tm
    row = lambda width: pl.BlockSpec((tm, width), lambda b, t: (b * nt + t, 0))
    mem = pl.BlockSpec((n_seq,) + mk.shape[1:], lambda b, t: (b, 0, 0))
    return pl.pallas_call(
        functools.partial(_mixmem_kernel, n_seq=n_seq),
        out_shape=jax.ShapeDtypeStruct((n, d), F32),
        grid=(batch // n_seq, nt),
        in_specs=[row(d), row(GDN_QK), row(SC_DIM), row(SB_W),
                  _const_spec(wmix.shape), _const_spec(g.shape), _const_spec(wq.shape),
                  _const_spec(qg.shape), mem, mem, _const_spec(wo.shape)],
        out_specs=row(d),
        compiler_params=_params(2),
        name="mixmem",
    )(x, ya, yb, yc, wmix, g, wq, qg, mk, mv, wo)


def _ffn_kernel(x_ref, g_ref, wg_ref, wu_ref, cw_ref, wd_ref, hist_ref, o_ref, tail_ref, *, n_seq, d_ff):
    @pl.when(pl.program_id(1) == 0)
    def _():
        tail_ref[...] = hist_ref[...]

    x = x_ref[...]
    h = _rms(x, g_ref[...]).astype(BF16)
    acc = x
    for c0 in range(0, d_ff, FF_CHUNK):
        cols = slice(c0, min(c0 + FF_CHUNK, d_ff))
        gate = _dot(h, wg_ref[:, cols])
        hist = tail_ref[:, :, cols]
        conv = (_shift_rows(gate, hist, 2) * cw_ref[0:1, cols]
                + _shift_rows(gate, hist, 1) * cw_ref[1:2, cols] + gate * cw_ref[2:3, cols])
        tail_ref[:, :, cols] = _last_rows(gate, n_seq)
        up = _dot(h, wu_ref[:, cols])
        hid = (_silu(conv) * up).astype(BF16)
        acc = acc + _dot(hid, wd_ref[cols, :])
    o_ref[...] = acc


def _ffn(x, g, wg, wu, cw, wd, hist, batch, tm, n_seq):
    n, d = x.shape
    d_ff = wg.shape[1]
    nt = n // (batch // n_seq) // tm
    row = pl.BlockSpec((tm, d), lambda b, t: (b * nt + t, 0))
    per_b = pl.BlockSpec((n_seq, HIST, d_ff), lambda b, t: (b, 0, 0))
    return pl.pallas_call(
        functools.partial(_ffn_kernel, n_seq=n_seq, d_ff=d_ff),
        out_shape=(jax.ShapeDtypeStruct((n, d), F32),
                   jax.ShapeDtypeStruct((batch, HIST, d_ff), F32)),
        grid=(batch // n_seq, nt),
        in_specs=[row, _const_spec(g.shape), _const_spec(wg.shape), _const_spec(wu.shape),
                  _const_spec(cw.shape), _const_spec(wd.shape), per_b],
        out_specs=(row, per_b),
        compiler_params=_params(2),
        name="ffn",
    )(x, g, wg, wu, cw, wd, hist)


def _mem_kv_kernel(m_ref, g_ref, wk_ref, wv_ref, kg_ref, k_ref, v_ref):
    m = _rms(m_ref[...], g_ref[...]).astype(BF16)
    k = _dot(m, wk_ref[...])
    kg = kg_ref[...]
    for hd in range(MEM_HEADS):
        lanes = slice(MEM_DH * hd, MEM_DH * (hd + 1))
        k_ref[:, lanes] = _rms(k[:, lanes], kg)
    v_ref[...] = _dot(m, wv_ref[...])


def _mem_kv(mem, g, wk, wv, kg, tm):
    n, d = mem.shape
    w = wk.shape[1]
    row = lambda width: pl.BlockSpec((tm, width), lambda i: (i, 0))
    return pl.pallas_call(
        _mem_kv_kernel,
        out_shape=(jax.ShapeDtypeStruct((n, w), F32), jax.ShapeDtypeStruct((n, w), F32)),
        grid=(n // tm,),
        in_specs=[row(d), _const_spec(g.shape), _const_spec(wk.shape), _const_spec(wv.shape),
                  _const_spec(kg.shape)],
        out_specs=(row(w), row(w)),
        compiler_params=_params(1),
        name="mem_kv",
    )(mem, g, wk, wv, kg)


def _pad_hist(buf):
    return jnp.pad(buf, ((0, 0), (HIST - buf.shape[1], 0), (0, 0)))


def _largest_tile(total, cap):
    t = min(total, cap)
    while total % t:
        t //= 2
    return t


def _layer(x, batch, lw, mem_k, mem_v, s0, hist_g, hist_s, hist_f, layer,
           kv_stack=None, sb_cache=None):
    n = x.shape[0]
    t_len = n // batch
    chunk = min(CHUNK, t_len)
    tt = _largest_tile(t_len, 8 * chunk)
    tm = _largest_tile(t_len, 1024)

    if kv_stack is not None:
        qkv, gz, gab, sbcx, cq, kt_all, vt_all, ktb, vtb = _in_proj_t(
            x, lw["norm_mix_g"], lw["w_in"], lw["w_kv_t"], kv_stack[0], kv_stack[1], layer, batch, tm)
        kv_out = (kt_all, vt_all)
    else:
        qkv, gz, gab, sbcx, cq, ck, cv = _in_proj(
            x, lw["norm_mix_g"], lw["w_in"], lw["w_kv_t"], _largest_tile(n, 512))
        kv_out = (ck, cv)

    ya, yb, s_new, tail_g, tail_s = _gdn(
        qkv, gz, gab, sbcx, hist_g, hist_s, s0, lw["gdn_conv_w"], lw["sc_conv_w"],
        lw["gate_params"], lw["gdn_norm_g"], lw["sc_norm_g"], batch, tt, chunk)

    if kv_stack is not None:
        tq = _largest_tile(t_len, 256)
        yc = _sb_attn(cq, ktb, vtb, ktb, vtb, lw["sb_norm_g"], batch, tq, tq, tq, None)
    else:
        tkd = 128
        new_t = lambda a: jnp.pad(jnp.swapaxes(a.reshape(batch, t_len, SB_W), 1, 2).astype(BF16),
                                  ((0, 0), (0, 0), (0, tkd - t_len)))
        p_len = sb_cache[0].shape[3]
        tk = _largest_tile(p_len, 256)
        yc = _sb_attn(cq, new_t(ck), new_t(cv), sb_cache[0], sb_cache[1], lw["sb_norm_g"],
                      batch, t_len, tkd, tk, p_len // tk, layer=layer)

    n_seq = max(1, min(batch, 128 // t_len))
    while batch % n_seq:
        n_seq -= 1
    tm_seq = tm * n_seq
    x = _mixmem(x, ya, yb, yc, lw["w_mix_out"], lw["norm_mem_g"], lw["w_mq"], lw["mq_norm_g"],
                mem_k, mem_v, lw["w_mo"], batch, tm_seq, n_seq)
    x, tail_f = _ffn(x, lw["norm_ffn_g"], lw["w_gate"], lw["w_up"], lw["ffn_conv_w"],
                     lw["w_down"], hist_f, batch, tm_seq, n_seq)
    return x, s_new, tail_g[:, HIST - 3:], tail_s[:, HIST - 2:], kv_out, tail_f[:, HIST - 2:]


def kernel(x_prompt, x_sample, mem_prompt, state_gdn, cache_gdn_conv, cache_sc_conv, cache_sb_k, cache_sb_v, cache_mem_k, cache_mem_v, cache_ffn_conv, norm_mix_g, w_in, gdn_conv_w, gdn_A_log, gdn_dt_bias, gdn_norm_g, sc_conv_w, sc_norm_g, sb_norm_g, w_mix_out, norm_mem_g, mem_in_norm_g, w_mq, w_mk, w_mv, mq_norm_g, mk_norm_g, w_mo, norm_ffn_g, w_gate, w_up, ffn_conv_w, w_down):
    depth = w_in.shape[0]
    bp, tp, d = x_prompt.shape
    bs, ts, _ = x_sample.shape
    n_mem = mem_prompt.shape[1]
    d_ff = w_gate.shape[2]
    p_len = cache_sb_k.shape[2]
    n_gate = 2 * GDN_QK + 2 * GDN_QK
    n_kv = w_in.shape[2] - 2 * SB_W

    xp = x_prompt.reshape(bp * tp, d)
    xs = x_sample.reshape(bs * ts, d)
    mem = mem_prompt.reshape(bp * n_mem, d)
    zeros = lambda b, c: jnp.zeros((b, HIST, c), F32)
    to_t = lambda c: jnp.transpose(c, (0, 1, 3, 4, 2)).reshape(depth, bs, SB_W, p_len)
    sb_cache = (to_t(cache_sb_k), to_t(cache_sb_v))
    kv_stack = (jnp.zeros((depth, bp, SB_W, tp), F32), jnp.zeros((depth, bp, SB_W, tp), F32))

    outs = [[] for _ in range(14)]
    for l in range(depth):
        w = w_in[l]
        w_packed = jnp.concatenate(
            [w[:, :n_gate + 2 * GDN_HEADS],
             jnp.zeros((d, 128 - 2 * GDN_HEADS), w.dtype),
             w[:, n_gate + 2 * GDN_HEADS:n_kv]], axis=1).astype(BF16)
        gate_params = jnp.zeros((8, 128), F32)
        gate_params = gate_params.at[0, GDN_HEADS:2 * GDN_HEADS].set(gdn_A_log[l])
        gate_params = gate_params.at[1, GDN_HEADS:2 * GDN_HEADS].set(gdn_dt_bias[l])
        row = lambda a: a[l].reshape(1, -1)
        lw = dict(
            norm_mix_g=row(norm_mix_g), w_in=w_packed, w_kv_t=w[:, n_kv:].T.astype(BF16),
            gdn_conv_w=gdn_conv_w[l],
            gate_params=gate_params, gdn_norm_g=row(gdn_norm_g), sc_conv_w=sc_conv_w[l],
            sc_norm_g=row(sc_norm_g), sb_norm_g=row(sb_norm_g),
            w_mix_out=w_mix_out[l].astype(BF16), norm_mem_g=row(norm_mem_g),
            w_mq=w_mq[l].astype(BF16), mq_norm_g=row(mq_norm_g), w_mo=w_mo[l].astype(BF16),
            norm_ffn_g=row(norm_ffn_g), w_gate=w_gate[l].astype(BF16), w_up=w_up[l].astype(BF16),
            ffn_conv_w=ffn_conv_w[l], w_down=w_down[l].astype(BF16))

        mk, mv = _mem_kv(mem, row(mem_in_norm_g), w_mk[l].astype(BF16), w_mv[l].astype(BF16),
                         row(mk_norm_g), _largest_tile(bp * n_mem, 512))
        mk3 = mk.reshape(bp, n_mem, MEM_HEADS * MEM_DH)
        mv3 = mv.reshape(bp, n_mem, MEM_HEADS * MEM_DH)
        xp, s_n, gc_n, sc_n, kv_stack, fc_n = _layer(
            xp, bp, lw, mk3, mv3, jnp.zeros((bp, GDN_HEADS, GDN_DK, GDN_DK), F32),
            zeros(bp, GDN_CONV_DIM), zeros(bp, SC_DIM), zeros(bp, d_ff), l, kv_stack=kv_stack)
        for i, a in ((0, s_n), (1, gc_n), (2, sc_n), (5, mk.reshape(bp, n_mem, MEM_HEADS, MEM_DH)),
                     (6, mv.reshape(bp, n_mem, MEM_HEADS, MEM_DH)), (7, fc_n)):
            outs[i].append(a)

        xs, s_n, gc_n, sc_n, (k_n, v_n), fc_n = _layer(
            xs, bs, lw,
            cache_mem_k[l].reshape(bs, n_mem, MEM_HEADS * MEM_DH),
            cache_mem_v[l].reshape(bs, n_mem, MEM_HEADS * MEM_DH),
            state_gdn[l], _pad_hist(cache_gdn_conv[l]), _pad_hist(cache_sc_conv[l]),
            _pad_hist(cache_ffn_conv[l]), l, sb_cache=sb_cache)
        for i, a in enumerate((s_n, gc_n, sc_n, k_n.reshape(bs, ts, SB_HEADS, SB_DH),
                               v_n.reshape(bs, ts, SB_HEADS, SB_DH), fc_n)):
            outs[8 + i].append(a)

    from_t = lambda a: jnp.transpose(a.reshape(depth, bp, SB_HEADS, SB_DH, tp), (0, 1, 4, 2, 3))
    stacked = [jnp.stack(o) if o else None for o in outs]
    stacked[3], stacked[4] = from_t(kv_stack[0]), from_t(kv_stack[1])
    return (xp.reshape(bp, tp, d), xs.reshape(bs, ts, d)) + tuple(stacked)
```

```python
import functools

import jax
import jax.numpy as jnp
from jax import lax
from jax.experimental import pallas as pl
from jax.experimental.pallas import tpu as pltpu

F32 = jnp.float32
BF16 = jnp.bfloat16
EPS = 1e-6

GDN_HEADS = 4
GDN_DK = 128
GDN_QK = 512
GDN_CONV_DIM = 1536
SC_DIM = 256
SB_HEADS = 4
SB_DH = 64
SB_W = 256
MEM_HEADS = 4
MEM_DH = 128
CHUNK = 64
HIST = 8
FF_CHUNK = 1024
SB_LOG_ZERO = -104.0
VMEM_LIMIT = 56 * 1024 * 1024

C_QKV, C_GZ, C_GAB, C_SBCX, C_CQ, C_CK = 0, 1536, 2048, 2176, 2944, 3200


def _dot(a, b, precision=None):
    return jnp.dot(a, b, preferred_element_type=F32, precision=precision)


def _dot_nt(a, b, precision=None):
    return lax.dot_general(a, b, (((1,), (1,)), ((), ())),
                           preferred_element_type=F32, precision=precision)


def _dot_tn(a, b, precision=None):
    return lax.dot_general(a, b, (((0,), (0,)), ((), ())),
                           preferred_element_type=F32, precision=precision)


def _rms(x, g):
    return x * lax.rsqrt(jnp.mean(x * x, axis=-1, keepdims=True) + EPS) * g


def _sigmoid(x):
    return 1.0 / (1.0 + jnp.exp(-x))


def _silu(x):
    return x * _sigmoid(x)


def _softplus(x):
    return jnp.maximum(x, 0.0) + jnp.log(1.0 + jnp.exp(-jnp.abs(x)))


def _shift_rows(x, hist, d):
    n_seq = hist.shape[0]
    seg = x.shape[0] // n_seq
    r = pltpu.roll(x, d, 0)
    row = lax.broadcasted_iota(jnp.int32, (HIST, x.shape[1]), 0)
    pieces = []
    for j in range(n_seq):
        pieces.append(jnp.where(row < d, pltpu.roll(hist[j], d, 0), r[j * seg:j * seg + HIST]))
        pieces.append(r[j * seg + HIST:(j + 1) * seg])
    return jnp.concatenate(pieces, axis=0)


def _last_rows(x, n_seq):
    seg = x.shape[0] // n_seq
    return jnp.stack([x[(j + 1) * seg - HIST:(j + 1) * seg] for j in range(n_seq)], axis=0)


def _group_mean_matrix(width, group):
    r = lax.broadcasted_iota(jnp.int32, (width, width), 0) // group
    c = lax.broadcasted_iota(jnp.int32, (width, width), 1) // group
    return jnp.where(r == c, 1.0 / group, 0.0).astype(BF16)


def _params(n_axes):
    return pltpu.CompilerParams(dimension_semantics=("arbitrary",) * n_axes,
                                vmem_limit_bytes=VMEM_LIMIT)


def _const_spec(shape):
    nd = len(shape)
    return pl.BlockSpec(shape, lambda *_: (0,) * nd, pipeline_mode=pl.Buffered(1))


def _l2norm_heads(a):
    parts = []
    for h in range(GDN_HEADS):
        ah = a[:, GDN_DK * h:GDN_DK * (h + 1)]
        parts.append(ah * lax.rsqrt(jnp.sum(ah * ah, axis=-1, keepdims=True) + EPS))
    return jnp.concatenate(parts, axis=1)


def _gdn_front(x, hist, cw_ref, part):
    cols = slice(GDN_QK * part, GDN_QK * (part + 1))
    conv = (_shift_rows(x, hist, 3) * cw_ref[0:1, cols] + _shift_rows(x, hist, 2) * cw_ref[1:2, cols]
            + _shift_rows(x, hist, 1) * cw_ref[2:3, cols] + x * cw_ref[3:4, cols])
    act = _silu(conv)
    if part == 0:
        return _l2norm_heads(act) * (GDN_DK ** -0.5)
    return _l2norm_heads(act) if part == 1 else act


def _short_conv(sbcx, hist, scw_ref, sng_ref):
    s_b = sbcx[:, 0:SC_DIM]
    pre = sbcx[:, SC_DIM:2 * SC_DIM] * sbcx[:, 2 * SC_DIM:3 * SC_DIM]
    u_c = (_shift_rows(pre, hist, 2) * scw_ref[0:1, :] + _shift_rows(pre, hist, 1) * scw_ref[1:2, :]
           + pre * scw_ref[2:3, :])
    yb = s_b * u_c
    ms = _dot2(yb * yb, _group_mean_matrix(SC_DIM, SC_DIM // 4))
    return (yb * lax.rsqrt(ms + EPS) * sng_ref[...]).astype(BF16), pre


def _in_proj_kernel(*refs, transposed_kv):
    if transposed_kv:
        (x_ref, g_ref, w_ref, wkv_ref, _, _, qkv_ref, gz_ref, gab_ref, sbcx_ref, cq_ref,
         kt_ref, vt_ref, ktb_ref, vtb_ref) = refs
    else:
        (x_ref, g_ref, w_ref, wkv_ref, qkv_ref, gz_ref, gab_ref, sbcx_ref, cq_ref,
         ck_ref, cv_ref) = refs
    h = _rms(x_ref[...], g_ref[...]).astype(BF16)

    def proj(c0, c1):
        return _dot(h, w_ref[:, c0:c1])

    for c in range(0, GDN_CONV_DIM, 512):
        qkv_ref[:, c:c + 512] = proj(C_QKV + c, C_QKV + c + 512)
    gz_ref[...] = proj(C_GZ, C_GAB)
    gab_ref[...] = proj(C_GAB, C_SBCX)
    for c in range(0, 3 * SC_DIM, 256):
        sbcx_ref[:, c:c + 256] = proj(C_SBCX + c, C_SBCX + c + 256)
    cq_ref[...] = (proj(C_CQ, C_CK) * (SB_DH ** -0.5)).astype(BF16)
    if transposed_kv:
        kt = _dot_nt(wkv_ref[0:SB_W, :], h)
        kt_ref[...] = kt
        ktb_ref[...] = kt.astype(BF16)
        vt = _dot_nt(wkv_ref[SB_W:2 * SB_W, :], h)
        vt_ref[...] = vt
        vtb_ref[...] = vt.astype(BF16)
    else:
        ck_ref[...] = _dot_nt(h, wkv_ref[0:SB_W, :])
        cv_ref[...] = _dot_nt(h, wkv_ref[SB_W:2 * SB_W, :])


def _in_proj_outs(n):
    return [jax.ShapeDtypeStruct((n, GDN_CONV_DIM), F32), jax.ShapeDtypeStruct((n, GDN_QK), F32),
            jax.ShapeDtypeStruct((n, 128), F32), jax.ShapeDtypeStruct((n, 3 * SC_DIM), F32),
            jax.ShapeDtypeStruct((n, SB_W), BF16)]


def _in_proj_t(x, g, w, wkv_t, kt_all, vt_all, layer, batch, tm):
    n, d = x.shape
    t_len = n // batch
    nt = t_len // tm
    row = lambda width: pl.BlockSpec((tm, width), lambda b, t: (b * nt + t, 0))
    stacked = pl.BlockSpec((None, None, SB_W, tm), lambda b, t: (layer, b, 0, t))
    per_layer = pl.BlockSpec((None, SB_W, tm), lambda b, t: (b, 0, t))
    any_spec = pl.BlockSpec(memory_space=pl.ANY)
    out_shape = _in_proj_outs(n) + [
        jax.ShapeDtypeStruct(kt_all.shape, F32), jax.ShapeDtypeStruct(vt_all.shape, F32),
        jax.ShapeDtypeStruct((batch, SB_W, t_len), BF16), jax.ShapeDtypeStruct((batch, SB_W, t_len), BF16)]
    return pl.pallas_call(
        functools.partial(_in_proj_kernel, transposed_kv=True),
        out_shape=tuple(out_shape),
        grid=(batch, nt),
        in_specs=[row(d), _const_spec((1, d)), _const_spec(w.shape), _const_spec(wkv_t.shape),
                  any_spec, any_spec],
        out_specs=(row(GDN_CONV_DIM), row(GDN_QK), row(128), row(3 * SC_DIM), row(SB_W),
                   stacked, stacked, per_layer, per_layer),
        input_output_aliases={4: 5, 5: 6},
        compiler_params=_params(2),
        name="in_proj",
    )(x, g, w, wkv_t, kt_all, vt_all)


def _in_proj(x, g, w, wkv_t, tm):
    n, d = x.shape
    row = lambda width: pl.BlockSpec((tm, width), lambda i: (i, 0))
    out_shape = _in_proj_outs(n) + [jax.ShapeDtypeStruct((n, SB_W), F32)] * 2
    return pl.pallas_call(
        functools.partial(_in_proj_kernel, transposed_kv=False),
        out_shape=tuple(out_shape),
        grid=(n // tm,),
        in_specs=[row(d), _const_spec((1, d)), _const_spec(w.shape), _const_spec(wkv_t.shape)],
        out_specs=(row(GDN_CONV_DIM), row(GDN_QK), row(128), row(3 * SC_DIM), row(SB_W),
                   row(SB_W), row(SB_W)),
        compiler_params=_params(1),
        name="in_proj",
    )(x, g, w, wkv_t)


def _split2(x):
    hi = x.astype(BF16)
    return hi, (x - hi.astype(F32)).astype(BF16)


def _dot2(a, b_exact):
    hi, lo = _split2(a)
    return _dot(hi, b_exact) + _dot(lo, b_exact)


def _dot2_left(a_exact, b):
    hi, lo = _split2(b)
    return _dot(a_exact, hi) + _dot(a_exact, lo)


def _gdn_kernel(qkv_ref, gz_ref, gab_ref, sbcx_ref, hg_ref, hs_ref, s0_ref,
                cw_ref, scw_ref, gp_ref, gng_ref, sng_ref,
                ya_ref, yb_ref, s_ref, tg_ref, ts_ref, carry_g, carry_s, *, tt, chunk):
    nh = GDN_HEADS
    sw = nh * chunk
    sec = max(sw, 128)
    n_chunks = tt // chunk

    @pl.when(pl.program_id(1) == 0)
    def _():
        carry_g[...] = hg_ref[...]
        carry_s[...] = hs_ref[...]
        s_ref[...] = s0_ref[...]

    x = qkv_ref[...]
    qn, kn, vv = [_gdn_front(x[:, GDN_QK * p:GDN_QK * (p + 1)], carry_g[:, :, GDN_QK * p:GDN_QK * (p + 1)],
                             cw_ref, p) for p in range(3)]
    carry_g[...] = _last_rows(x, 1)
    tg_ref[...] = _last_rows(x, 1)
    yb, pre_s = _short_conv(sbcx_ref[...], carry_s[...], scw_ref, sng_ref)
    yb_ref[...] = yb
    carry_s[...] = _last_rows(pre_s, 1)
    ts_ref[...] = _last_rows(pre_s, 1)

    slab = gab_ref[...]
    beta_s = _sigmoid(slab)
    g_s = -jnp.exp(gp_ref[0:1, :]) * _softplus(slab + gp_ref[1:2, :])
    ri = lax.broadcasted_iota(jnp.int32, (tt, tt), 0)
    ci = lax.broadcasted_iota(jnp.int32, (tt, tt), 1)
    tri = jnp.where(jnp.logical_and(ri // chunk == ci // chunk, ri >= ci), 1.0, 0.0).astype(BF16)
    gc = _dot2_left(tri, g_s)
    lane128 = lax.broadcasted_iota(jnp.int32, (1, 128), 1)
    slab2 = jnp.where(lane128 < nh, beta_s, gc)
    ew = 2 * GDN_QK + 2 * sec
    er = lax.broadcasted_iota(jnp.int32, (128, ew), 0)
    ec = lax.broadcasted_iota(jnp.int32, (128, ew), 1)
    src_row = jnp.where(ec < GDN_QK, ec // GDN_DK,
              jnp.where(ec < 2 * GDN_QK, nh + (ec - GDN_QK) // GDN_DK,
              jnp.where(ec < 2 * GDN_QK + sec, (ec - 2 * GDN_QK) // chunk,
                        nh + (ec - 2 * GDN_QK - sec) // chunk)))
    expand = jnp.where(er == src_row, 1.0, 0.0).astype(BF16)
    xp = _dot2(slab2, expand)
    b512 = xp[:, 0:GDN_QK]
    gc512 = xp[:, GDN_QK:2 * GDN_QK]
    b_s = xp[:, 2 * GDN_QK:2 * GDN_QK + sec][:, 0:sw]
    gc_s = xp[:, 2 * GDN_QK + sec:2 * GDN_QK + 2 * sec][:, 0:sw]
    eg512 = jnp.exp(gc512)

    qb = qn.astype(BF16)
    kb = kn.astype(BF16)
    qe = (qn * eg512).astype(BF16)
    rv = vv * b512
    rk = kn * (b512 * eg512)

    r_s = lax.broadcasted_iota(jnp.int32, (chunk, sw), 0)
    c_s = lax.broadcasted_iota(jnp.int32, (chunk, sw), 1)
    c_in = c_s - (c_s // chunk) * chunk
    causal_s = r_s >= c_in
    strict_s = r_s > c_in
    eye_s = jnp.where(r_s == c_in, 1.0, 0.0).astype(F32)
    pair_mask = jnp.logical_and(r_s // 2 == c_in // 2, strict_s)
    level_masks = []
    m = 2
    while m < chunk:
        level_masks.append(jnp.logical_and(
            jnp.logical_and(r_s // (2 * m) == c_in // (2 * m), r_s // m != c_in // m), strict_s))
        m *= 2
    bd_r = lax.broadcasted_iota(jnp.int32, (sw, sw), 0) // chunk
    bd_c = lax.broadcasted_iota(jnp.int32, (sw, sw), 1) // chunk
    bd_mask = bd_r == bd_c
    bdw_r = lax.broadcasted_iota(jnp.int32, (sw, GDN_QK), 0) // chunk
    bdw_c = lax.broadcasted_iota(jnp.int32, (sw, GDN_QK), 1) // GDN_DK
    bdw_mask = bdw_r == bdw_c
    zero_b = jnp.zeros((), BF16)

    def bd(a):
        return jnp.where(bd_mask, jnp.concatenate([a] * nh, axis=0), zero_b)

    def bdw(a):
        return jnp.where(bdw_mask, jnp.concatenate([a] * nh, axis=0), zero_b)

    tr = lax.broadcasted_iota(jnp.int32, (tt, sw), 0)
    tc = lax.broadcasted_iota(jnp.int32, (tt, sw), 1)
    eye_t = tr - (tr // chunk) * chunk == tc - (tc // chunk) * chunk
    same_chunk = jnp.where(ri // chunk == ci // chunk, 1.0, 0.0).astype(BF16)
    row_all = _dot2_left(same_chunk, jnp.where(eye_t, gc_s, 0.0))

    l_b, inv, qkd = [], [], []
    for c in range(n_chunks):
        rows = slice(c * chunk, (c + 1) * chunk)
        kq = _dot_nt(jnp.concatenate([kb[rows], qb[rows]], axis=0), bdw(kb[rows]))
        kk = kq[0:chunk]
        qk = kq[chunk:2 * chunk]
        decay = jnp.exp(jnp.where(causal_s, gc_s[rows] - row_all[rows], -jnp.inf))
        lm = jnp.where(strict_s, b_s[rows] * kk * decay, 0.0)
        l_b.append(lm.astype(BF16))
        inv.append(eye_s - jnp.where(pair_mask, lm, 0.0))
        qkd.append((qk * decay).astype(BF16))

    for mask in level_masks:
        for c in range(n_chunks):
            x_b = inv[c].astype(BF16)
            cx = _dot(jnp.where(mask, l_b[c], zero_b), bd(x_b))
            inv[c] = inv[c] - _dot(x_b, bd(cx.astype(BF16)))

    u_all, w_all = [], []
    for c in range(n_chunks):
        rows = slice(c * chunk, (c + 1) * chunk)
        x_b = inv[c].astype(BF16)
        u_all.append(_dot(x_b, bdw(rv[rows].astype(BF16))))
        w_all.append(_dot(x_b, bdw(rk[rows].astype(BF16))).astype(BF16))

    gng = gng_ref[...]
    for c in range(n_chunks):
        rows = slice(c * chunk, (c + 1) * chunk)
        g_last = gc512[(c + 1) * chunk - 1:(c + 1) * chunk, :]
        kd = (kn[rows] * jnp.exp(g_last - gc512[rows])).astype(BF16)
        eg_last = jnp.exp(g_last)
        s_old = [s_ref[h] for h in range(nh)]
        s_b = [s.astype(BF16) for s in s_old]
        wq = [_dot(jnp.concatenate([w_all[c][:, GDN_DK * h:GDN_DK * (h + 1)],
                                    qe[rows][:, GDN_DK * h:GDN_DK * (h + 1)]], axis=0), s_b[h])
              for h in range(nh)]
        v_new = u_all[c] - jnp.concatenate([wq[h][0:chunk] for h in range(nh)], axis=1)
        vb = v_new.astype(BF16)
        o_state = jnp.concatenate([wq[h][chunk:2 * chunk] for h in range(nh)], axis=1)
        o = o_state + _dot(qkd[c], bdw(vb))
        for h in range(nh):
            lanes = slice(GDN_DK * h, GDN_DK * (h + 1))
            s_ref[h] = s_old[h] * eg_last[:, lanes] + _dot_tn(kd[:, lanes], vb[:, lanes])
        z = gz_ref[c * chunk:(c + 1) * chunk, :]
        ya_ref[c * chunk:(c + 1) * chunk, :] = (jnp.concatenate(
            [_rms(o[:, GDN_DK * h:GDN_DK * (h + 1)], gng) for h in range(nh)], axis=1)
            * _silu(z)).astype(BF16)


def _gdn(qkv, gz, gab, sbcx, hist_g, hist_s, s0, cw, scw, gp, gng, sng, batch, tt, chunk):
    n = qkv.shape[0]
    nt = n // batch // tt
    row = lambda width: pl.BlockSpec((tt, width), lambda b, t: (b * nt + t, 0))
    per_b3 = lambda width: pl.BlockSpec((1, HIST, width), lambda b, t: (b, 0, 0))
    state = pl.BlockSpec((None, GDN_HEADS, GDN_DK, GDN_DK), lambda b, t: (b, 0, 0, 0))
    out_shape = (
        jax.ShapeDtypeStruct((n, GDN_QK), BF16),
        jax.ShapeDtypeStruct((n, SC_DIM), BF16),
        jax.ShapeDtypeStruct((batch, GDN_HEADS, GDN_DK, GDN_DK), F32),
        jax.ShapeDtypeStruct((batch, HIST, GDN_CONV_DIM), F32),
        jax.ShapeDtypeStruct((batch, HIST, SC_DIM), F32),
    )
    return pl.pallas_call(
        functools.partial(_gdn_kernel, tt=tt, chunk=chunk),
        out_shape=out_shape,
        grid=(batch, nt),
        in_specs=[row(GDN_CONV_DIM), row(GDN_QK), row(128), row(3 * SC_DIM),
                  per_b3(GDN_CONV_DIM), per_b3(SC_DIM), state,
                  _const_spec(cw.shape), _const_spec(scw.shape), _const_spec(gp.shape),
                  _const_spec(gng.shape), _const_spec(sng.shape)],
        out_specs=(row(GDN_QK), row(SC_DIM), state, per_b3(GDN_CONV_DIM), per_b3(SC_DIM)),
        scratch_shapes=[pltpu.VMEM((1, HIST, GDN_CONV_DIM), F32), pltpu.VMEM((1, HIST, SC_DIM), F32)],
        compiler_params=_params(2),
        name="gdn",
    )(qkv, gz, gab, sbcx, hist_g, hist_s, s0, cw, scw, gp, gng, sng)


def _sb_kernel(q_ref, kd_ref, vd_ref, kp_ref, vp_ref, g_ref, y_ref, acc_ref, c_ref,
               *, tq, tkd, tk, n_past_static):
    lane = lax.broadcasted_iota(jnp.int32, (1, SB_W), 1) // SB_DH
    rowh = lax.broadcasted_iota(jnp.int32, (SB_W, 1), 0) // SB_DH
    q = q_ref[...]
    zero_b = jnp.zeros((), BF16)
    q_heads = [jnp.where(lane == h, q, zero_b) for h in range(SB_HEADS)]

    acc_ref[...] = jnp.zeros_like(acc_ref)
    c_ref[...] = jnp.zeros_like(c_ref)

    def later_matrix(width):
        r = lax.broadcasted_iota(jnp.int32, (width, width), 0)
        c = lax.broadcasted_iota(jnp.int32, (width, width), 1)
        return jnp.where(r > c, 1.0, 0.0).astype(BF16)

    def block(kt, vt, later, mask):
        pv = jnp.zeros((tq, SB_W), F32)
        for h in range(SB_HEADS):
            z = _dot(q_heads[h], kt)
            sp = _softplus(z)
            spm = sp if mask is None else jnp.where(mask, sp, 0.0)
            rc = _dot2(spm, later)
            carry = c_ref[:, h:h + 1]
            a = jnp.exp((z - sp) - rc + carry)
            if mask is not None:
                a = jnp.where(mask, a, 0.0)
            pv = pv + _dot_nt(a.astype(BF16), jnp.where(rowh == h, vt, zero_b))
            c_ref[:, h:h + 1] = carry - (rc[:, 0:1] + spm[:, 0:1])
        acc_ref[...] += pv

    rq = lax.broadcasted_iota(jnp.int32, (tq, tkd), 0)
    ck = lax.broadcasted_iota(jnp.int32, (tq, tkd), 1)
    block(kd_ref[...], vd_ref[...], later_matrix(tkd), ck < rq)

    if n_past_static is None:
        n_past = pl.program_id(1) * (tq // tk)
    else:
        n_past = n_past_static
    later_past = later_matrix(tk)

    def cond(state):
        j, live = state
        return jnp.logical_and(j >= 0, live)

    def body(state):
        j, _ = state
        start = pl.multiple_of(j * tk, tk)
        kt = kp_ref[:, pl.ds(start, tk)].astype(BF16)
        vt = vp_ref[:, pl.ds(start, tk)].astype(BF16)
        block(kt, vt, later_past, None)
        live = jnp.max(c_ref[:, 0:SB_HEADS]) > SB_LOG_ZERO
        return j - 1, live

    lax.while_loop(cond, body, (n_past - 1, jnp.bool_(True)))

    o = acc_ref[...]
    ms = _dot2(o * o, _group_mean_matrix(SB_W, SB_DH))
    y_ref[...] = (o * lax.rsqrt(ms + EPS) * g_ref[...]).astype(BF16)


def _sb_attn(q, kd, vd, kp, vp, g, batch, tq, tkd, tk, n_past_static, layer=None):
    n = q.shape[0]
    nq = n // batch // tq
    qrow = pl.BlockSpec((tq, SB_W), lambda b, i: (b * nq + i, 0))
    if kd.shape[2] == tkd:
        diag = pl.BlockSpec((None, SB_W, tkd), lambda b, i: (b, 0, 0))
    else:
        diag = pl.BlockSpec((None, SB_W, tkd), lambda b, i: (b, 0, i))
    if layer is None:
        past = pl.BlockSpec((None, SB_W, kp.shape[2]), lambda b, i: (b, 0, 0))
    else:
        past = pl.BlockSpec((None, None, SB_W, kp.shape[3]), lambda b, i: (layer, b, 0, 0))
    return pl.pallas_call(
        functools.partial(_sb_kernel, tq=tq, tkd=tkd, tk=tk, n_past_static=n_past_static),
        out_shape=jax.ShapeDtypeStruct((n, SB_W), BF16),
        grid=(batch, nq),
        in_specs=[qrow, diag, diag, past, past, _const_spec(g.shape)],
        out_specs=qrow,
        scratch_shapes=[pltpu.VMEM((tq, SB_W), F32), pltpu.VMEM((tq, 128), F32)],
        compiler_params=_params(2),
        name="sb_attn",
    )(q, kd, vd, kp, vp, g)


def _mixmem_kernel(x_ref, ya_ref, yb_ref, yc_ref, wmix_ref, g_ref, wq_ref, qg_ref,
                   mk_ref, mv_ref, wo_ref, o_ref, *, n_seq):
    x1 = (x_ref[...]
          + _dot(ya_ref[...], wmix_ref[0:GDN_QK, :])
          + _dot(yb_ref[...], wmix_ref[GDN_QK:GDN_QK + SC_DIM, :])
          + _dot(yc_ref[...], wmix_ref[GDN_QK + SC_DIM:GDN_QK + SC_DIM + SB_W, :]))
    h = _rms(x1, g_ref[...]).astype(BF16)
    qm = _dot(h, wq_ref[...])
    qg = qg_ref[...]
    seg = qm.shape[0] // n_seq
    rows_out = []
    for j in range(n_seq):
        outs = []
        for hd in range(MEM_HEADS):
            lanes = slice(MEM_DH * hd, MEM_DH * (hd + 1))
            qh = _rms(qm[j * seg:(j + 1) * seg, lanes], qg).astype(BF16)
            kh = mk_ref[j, :, lanes].astype(BF16)
            vh = mv_ref[j, :, lanes].astype(BF16)
            s = _dot_nt(qh, kh) * (MEM_DH ** -0.5)
            e = jnp.exp(s - jnp.max(s, axis=-1, keepdims=True))
            p = e / jnp.sum(e, axis=-1, keepdims=True)
            outs.append(_dot(p.astype(BF16), vh).astype(BF16))
        rows_out.append(jnp.concatenate(outs, axis=1))
    om = jnp.concatenate(rows_out, axis=0)
    o_ref[...] = x1 + _dot(om, wo_ref[...])


def _mixmem(x, ya, yb, yc, wmix, g, wq, qg, mk, mv, wo, batch, tm, n_seq):
    n, d = x.shape
    nt = n // (batch // n_seq) // tm
    row = lambda width: pl.BlockSpec((tm, width), lambda b, t: (b * nt + t, 0))
    mem = pl.BlockSpec((n_seq,) + mk.shape[1:], lambda b, t: (b, 0, 0))
    return pl.pallas_call(
        functools.partial(_mixmem_kernel, n_seq=n_seq),
        out_shape=jax.ShapeDtypeStruct((n, d), F32),
        grid=(batch // n_seq, nt),
        in_specs=[row(d), row(GDN_QK), row(SC_DIM), row(SB_W),
                  _const_spec(wmix.shape), _const_spec(g.shape), _const_spec(wq.shape),
                  _const_spec(qg.shape), mem, mem, _const_spec(wo.shape)],
        out_specs=row(d),
        compiler_params=_params(2),
        name="mixmem",
    )(x, ya, yb, yc, wmix, g, wq, qg, mk, mv, wo)


def _ffn_kernel(x_ref, g_ref, wg_ref, wu_ref, cw_ref, wd_ref, hist_ref, o_ref, tail_ref, *, n_seq, d_ff):
    @pl.when(pl.program_id(1) == 0)
    def _():
        tail_ref[...] = hist_ref[...]

    x = x_ref[...]
    h = _rms(x, g_ref[...]).astype(BF16)
    acc = x
    for c0 in range(0, d_ff, FF_CHUNK):
        cols = slice(c0, min(c0 + FF_CHUNK, d_ff))
        gate = _dot(h, wg_ref[:, cols])
        hist = tail_ref[:, :, cols]
        conv = (_shift_rows(gate, hist, 2) * cw_ref[0:1, cols]
                + _shift_rows(gate, hist, 1) * cw_ref[1:2, cols] + gate * cw_ref[2:3, cols])
        tail_ref[:, :, cols] = _last_rows(gate, n_seq)
        up = _dot(h, wu_ref[:, cols])
        hid = (_silu(conv) * up).astype(BF16)
        acc = acc + _dot(hid, wd_ref[cols, :])
    o_ref[...] = acc


def _ffn(x, g, wg, wu, cw, wd, hist, batch, tm, n_seq):
    n, d = x.shape
    d_ff = wg.shape[1]
    nt = n // (batch // n_seq) // tm
    row = pl.BlockSpec((tm, d), lambda b, t: (b * nt + t, 0))
    per_b = pl.BlockSpec((n_seq, HIST, d_ff), lambda b, t: (b, 0, 0))
    return pl.pallas_call(
        functools.partial(_ffn_kernel, n_seq=n_seq, d_ff=d_ff),
        out_shape=(jax.ShapeDtypeStruct((n, d), F32),
                   jax.ShapeDtypeStruct((batch, HIST, d_ff), F32)),
        grid=(batch // n_seq, nt),
        in_specs=[row, _const_spec(g.shape), _const_spec(wg.shape), _const_spec(wu.shape),
                  _const_spec(cw.shape), _const_spec(wd.shape), per_b],
        out_specs=(row, per_b),
        compiler_params=_params(2),
        name="ffn",
    )(x, g, wg, wu, cw, wd, hist)


def _mem_kv_kernel(m_ref, g_ref, wk_ref, wv_ref, kg_ref, k_ref, v_ref):
    m = _rms(m_ref[...], g_ref[...]).astype(BF16)
    k = _dot(m, wk_ref[...])
    kg = kg_ref[...]
    for hd in range(MEM_HEADS):
        lanes = slice(MEM_DH * hd, MEM_DH * (hd + 1))
        k_ref[:, lanes] = _rms(k[:, lanes], kg)
    v_ref[...] = _dot(m, wv_ref[...])


def _mem_kv(mem, g, wk, wv, kg, tm):
    n, d = mem.shape
    w = wk.shape[1]
    row = lambda width: pl.BlockSpec((tm, width), lambda i: (i, 0))
    return pl.pallas_call(
        _mem_kv_kernel,
        out_shape=(jax.ShapeDtypeStruct((n, w), F32), jax.ShapeDtypeStruct((n, w), F32)),
        grid=(n // tm,),
        in_specs=[row(d), _const_spec(g.shape), _const_spec(wk.shape), _const_spec(wv.shape),
                  _const_spec(kg.shape)],
        out_specs=(row(w), row(w)),
        compiler_params=_params(1),
        name="mem_kv",
    )(mem, g, wk, wv, kg)


def _pad_hist(buf):
    return jnp.pad(buf, ((0, 0), (HIST - buf.shape[1], 0), (0, 0)))


def _largest_tile(total, cap):
    t = min(total, cap)
    while total % t:
        t //= 2
    return t


def _layer(x, batch, lw, mem_k, mem_v, s0, hist_g, hist_s, hist_f, layer,
           kv_stack=None, sb_cache=None):
    n = x.shape[0]
    t_len = n // batch
    chunk = min(CHUNK, t_len)
    tt = _largest_tile(t_len, 8 * chunk)
    tm = _largest_tile(t_len, 1024)

    if kv_stack is not None:
        qkv, gz, gab, sbcx, cq, kt_all, vt_all, ktb, vtb = _in_proj_t(
            x, lw["norm_mix_g"], lw["w_in"], lw["w_kv_t"], kv_stack[0], kv_stack[1], layer, batch, tm)
        kv_out = (kt_all, vt_all)
    else:
        qkv, gz, gab, sbcx, cq, ck, cv = _in_proj(
            x, lw["norm_mix_g"], lw["w_in"], lw["w_kv_t"], _largest_tile(n, 512))
        kv_out = (ck, cv)

    ya, yb, s_new, tail_g, tail_s = _gdn(
        qkv, gz, gab, sbcx, hist_g, hist_s, s0, lw["gdn_conv_w"], lw["sc_conv_w"],
        lw["gate_params"], lw["gdn_norm_g"], lw["sc_norm_g"], batch, tt, chunk)

    if kv_stack is not None:
        tq = _largest_tile(t_len, 256)
        yc = _sb_attn(cq, ktb, vtb, ktb, vtb, lw["sb_norm_g"], batch, tq, tq, tq, None)
    else:
        tkd = 128
        new_t = lambda a: jnp.pad(jnp.swapaxes(a.reshape(batch, t_len, SB_W), 1, 2).astype(BF16),
                                  ((0, 0), (0, 0), (0, tkd - t_len)))
        p_len = sb_cache[0].shape[3]
        tk = _largest_tile(p_len, 256)
        yc = _sb_attn(cq, new_t(ck), new_t(cv), sb_cache[0], sb_cache[1], lw["sb_norm_g"],
                      batch, t_len, tkd, tk, p_len // tk, layer=layer)

    n_seq = max(1, min(batch, 128 // t_len))
    while batch % n_seq:
        n_seq -= 1
    tm_seq = tm * n_seq
    x = _mixmem(x, ya, yb, yc, lw["w_mix_out"], lw["norm_mem_g"], lw["w_mq"], lw["mq_norm_g"],
                mem_k, mem_v, lw["w_mo"], batch, tm_seq, n_seq)
    x, tail_f = _ffn(x, lw["norm_ffn_g"], lw["w_gate"], lw["w_up"], lw["ffn_conv_w"],
                     lw["w_down"], hist_f, batch, tm_seq, n_seq)
    return x, s_new, tail_g[:, HIST - 3:], tail_s[:, HIST - 2:], kv_out, tail_f[:, HIST - 2:]


def kernel(x_prompt, x_sample, mem_prompt, state_gdn, cache_gdn_conv, cache_sc_conv, cache_sb_k, cache_sb_v, cache_mem_k, cache_mem_v, cache_ffn_conv, norm_mix_g, w_in, gdn_conv_w, gdn_A_log, gdn_dt_bias, gdn_norm_g, sc_conv_w, sc_norm_g, sb_norm_g, w_mix_out, norm_mem_g, mem_in_norm_g, w_mq, w_mk, w_mv, mq_norm_g, mk_norm_g, w_mo, norm_ffn_g, w_gate, w_up, ffn_conv_w, w_down):
    depth = w_in.shape[0]
    bp, tp, d = x_prompt.shape
    bs, ts, _ = x_sample.shape
    n_mem = mem_prompt.shape[1]
    d_ff = w_gate.shape[2]
    p_len = cache_sb_k.shape[2]
    n_gate = 2 * GDN_QK + 2 * GDN_QK
    n_kv = w_in.shape[2] - 2 * SB_W

    xp = x_prompt.reshape(bp * tp, d)
    xs = x_sample.reshape(bs * ts, d)
    mem = mem_prompt.reshape(bp * n_mem, d)
    zeros = lambda b, c: jnp.zeros((b, HIST, c), F32)
    to_t = lambda c: jnp.transpose(c, (0, 1, 3, 4, 2)).reshape(depth, bs, SB_W, p_len)
    sb_cache = (to_t(cache_sb_k), to_t(cache_sb_v))
    kv_stack = (jnp.zeros((depth, bp, SB_W, tp), F32), jnp.zeros((depth, bp, SB_W, tp), F32))

    outs = [[] for _ in range(14)]
    for l in range(depth):
        w = w_in[l]
        w_packed = jnp.concatenate(
            [w[:, :n_gate + 2 * GDN_HEADS],
             jnp.zeros((d, 128 - 2 * GDN_HEADS), w.dtype),
             w[:, n_gate + 2 * GDN_HEADS:n_kv]], axis=1).astype(BF16)
        gate_params = jnp.zeros((8, 128), F32)
        gate_params = gate_params.at[0, GDN_HEADS:2 * GDN_HEADS].set(gdn_A_log[l])
        gate_params = gate_params.at[1, GDN_HEADS:2 * GDN_HEADS].set(gdn_dt_bias[l])
        row = lambda a: a[l].reshape(1, -1)
        lw = dict(
            norm_mix_g=row(norm_mix_g), w_in=w_packed, w_kv_t=w[:, n_kv:].T.astype(BF16),
            gdn_conv_w=gdn_conv_w[l],
            gate_params=gate_params, gdn_norm_g=row(gdn_norm_g), sc_conv_w=sc_conv_w[l],
            sc_norm_g=row(sc_norm_g), sb_norm_g=row(sb_norm_g),
            w_mix_out=w_mix_out[l].astype(BF16), norm_mem_g=row(norm_mem_g),
            w_mq=w_mq[l].astype(BF16), mq_norm_g=row(mq_norm_g), w_mo=w_mo[l].astype(BF16),
            norm_ffn_g=row(norm_ffn_g), w_gate=w_gate[l].astype(BF16), w_up=w_up[l].astype(BF16),
            ffn_conv_w=ffn_conv_w[l], w_down=w_down[l].astype(BF16))

        mk, mv = _mem_kv(mem, row(mem_in_norm_g), w_mk[l].astype(BF16), w_mv[l].astype(BF16),
                         row(mk_norm_g), _largest_tile(bp * n_mem, 512))
        mk3 = mk.reshape(bp, n_mem, MEM_HEADS * MEM_DH)
        mv3 = mv.reshape(bp, n_mem, MEM_HEADS * MEM_DH)
        xp, s_n, gc_n, sc_n, kv_stack, fc_n = _layer(
            xp, bp, lw, mk3, mv3, jnp.zeros((bp, GDN_HEADS, GDN_DK, GDN_DK), F32),
            zeros(bp, GDN_CONV_DIM), zeros(bp, SC_DIM), zeros(bp, d_ff), l, kv_stack=kv_stack)
        for i, a in ((0, s_n), (1, gc_n), (2, sc_n), (5, mk.reshape(bp, n_mem, MEM_HEADS, MEM_DH)),
                     (6, mv.reshape(bp, n_mem, MEM_HEADS, MEM_DH)), (7, fc_n)):
            outs[i].append(a)

        xs, s_n, gc_n, sc_n, (k_n, v_n), fc_n = _layer(
            xs, bs, lw,
            cache_mem_k[l].reshape(bs, n_mem, MEM_HEADS * MEM_DH),
            cache_mem_v[l].reshape(bs, n_mem, MEM_HEADS * MEM_DH),
            state_gdn[l], _pad_hist(cache_gdn_conv[l]), _pad_hist(cache_sc_conv[l]),
            _pad_hist(cache_ffn_conv[l]), l, sb_cache=sb_cache)
        for i, a in enumerate((s_n, gc_n, sc_n, k_n.reshape(bs, ts, SB_HEADS, SB_DH),
                               v_n.reshape(bs, ts, SB_HEADS, SB_DH), fc_n)):
            outs[8 + i].append(a)

    from_t = lambda a: jnp.transpose(a.reshape(depth, bp, SB_HEADS, SB_DH, tp), (0, 1, 4, 2, 3))
    stacked = [jnp.stack(o) if o else None for o in outs]
    stacked[3], stacked[4] = from_t(kv_stack[0]), from_t(kv_stack[1])
    return (xp.reshape(bp, tp, d), xs.reshape(bs, ts, d)) + tuple(stacked)
```

```python
import functools

import jax
import jax.numpy as jnp
from jax import lax
from jax.experimental import pallas as pl
from jax.experimental.pallas import tpu as pltpu

F32 = jnp.float32
BF16 = jnp.bfloat16
EPS = 1e-6

GDN_HEADS = 4
GDN_DK = 128
GDN_QK = 512
GDN_CONV_DIM = 1536
SC_DIM = 256
SB_HEADS = 4
SB_DH = 64
SB_W = 256
MEM_HEADS = 4
MEM_DH = 128
CHUNK = 64
HIST = 8
FF_CHUNK = 1024
SB_LOG_ZERO = -104.0
VMEM_LIMIT = 56 * 1024 * 1024

C_QKV, C_GZ, C_GAB, C_SBCX, C_CQ, C_CK = 0, 1536, 2048, 2176, 2944, 3200


def _dot(a, b, precision=None):
    return jnp.dot(a, b, preferred_element_type=F32, precision=precision)


def _dot_nt(a, b, precision=None):
    return lax.dot_general(a, b, (((1,), (1,)), ((), ())),
                           preferred_element_type=F32, precision=precision)


def _dot_tn(a, b, precision=None):
    return lax.dot_general(a, b, (((0,), (0,)), ((), ())),
                           preferred_element_type=F32, precision=precision)


def _rms(x, g):
    return x * lax.rsqrt(jnp.mean(x * x, axis=-1, keepdims=True) + EPS) * g


def _sigmoid(x):
    return 1.0 / (1.0 + jnp.exp(-x))


def _silu(x):
    return x * _sigmoid(x)


def _softplus(x):
    return jnp.maximum(x, 0.0) + jnp.log(1.0 + jnp.exp(-jnp.abs(x)))


def _shift_rows(x, hist, d):
    n_seq = hist.shape[0]
    seg = x.shape[0] // n_seq
    r = pltpu.roll(x, d, 0)
    row = lax.broadcasted_iota(jnp.int32, (HIST, x.shape[1]), 0)
    pieces = []
    for j in range(n_seq):
        pieces.append(jnp.where(row < d, pltpu.roll(hist[j], d, 0), r[j * seg:j * seg + HIST]))
        pieces.append(r[j * seg + HIST:(j + 1) * seg])
    return jnp.concatenate(pieces, axis=0)


def _last_rows(x, n_seq):
    seg = x.shape[0] // n_seq
    return jnp.stack([x[(j + 1) * seg - HIST:(j + 1) * seg] for j in range(n_seq)], axis=0)


def _group_mean_matrix(width, group):
    r = lax.broadcasted_iota(jnp.int32, (width, width), 0) // group
    c = lax.broadcasted_iota(jnp.int32, (width, width), 1) // group
    return jnp.where(r == c, 1.0 / group, 0.0).astype(BF16)


def _params(n_axes):
    return pltpu.CompilerParams(dimension_semantics=("arbitrary",) * n_axes,
                                vmem_limit_bytes=VMEM_LIMIT)


def _const_spec(shape):
    nd = len(shape)
    return pl.BlockSpec(shape, lambda *_: (0,) * nd, pipeline_mode=pl.Buffered(1))


def _l2norm_heads(a):
    parts = []
    for h in range(GDN_HEADS):
        ah = a[:, GDN_DK * h:GDN_DK * (h + 1)]
        parts.append(ah * lax.rsqrt(jnp.sum(ah * ah, axis=-1, keepdims=True) + EPS))
    return jnp.concatenate(parts, axis=1)


def _gdn_front(x, hist, cw_ref, part):
    cols = slice(GDN_QK * part, GDN_QK * (part + 1))
    conv = (_shift_rows(x, hist, 3) * cw_ref[0:1, cols] + _shift_rows(x, hist, 2) * cw_ref[1:2, cols]
            + _shift_rows(x, hist, 1) * cw_ref[2:3, cols] + x * cw_ref[3:4, cols])
    act = _silu(conv)
    if part == 0:
        return _l2norm_heads(act) * (GDN_DK ** -0.5)
    return _l2norm_heads(act) if part == 1 else act


def _short_conv(sbcx, hist, scw_ref, sng_ref):
    s_b = sbcx[:, 0:SC_DIM]
    pre = sbcx[:, SC_DIM:2 * SC_DIM] * sbcx[:, 2 * SC_DIM:3 * SC_DIM]
    u_c = (_shift_rows(pre, hist, 2) * scw_ref[0:1, :] + _shift_rows(pre, hist, 1) * scw_ref[1:2, :]
           + pre * scw_ref[2:3, :])
    yb = s_b * u_c
    ms = _dot2(yb * yb, _group_mean_matrix(SC_DIM, SC_DIM // 4))
    return (yb * lax.rsqrt(ms + EPS) * sng_ref[...]).astype(BF16), pre


def _in_proj_kernel(*refs, transposed_kv):
    if transposed_kv:
        (x_ref, g_ref, w_ref, wkv_ref, _, _, qkv_ref, gz_ref, gab_ref, sbcx_ref, cq_ref,
         kt_ref, vt_ref, ktb_ref, vtb_ref) = refs
    else:
        (x_ref, g_ref, w_ref, wkv_ref, qkv_ref, gz_ref, gab_ref, sbcx_ref, cq_ref,
         ck_ref, cv_ref) = refs
    h = _rms(x_ref[...], g_ref[...]).astype(BF16)

    def proj(c0, c1):
        return _dot(h, w_ref[:, c0:c1])

    for c in range(0, GDN_CONV_DIM, 512):
        qkv_ref[:, c:c + 512] = proj(C_QKV + c, C_QKV + c + 512)
    gz_ref[...] = proj(C_GZ, C_GAB)
    gab_ref[...] = proj(C_GAB, C_SBCX)
    for c in range(0, 3 * SC_DIM, 256):
        sbcx_ref[:, c:c + 256] = proj(C_SBCX + c, C_SBCX + c + 256)
    cq_ref[...] = (proj(C_CQ, C_CK) * (SB_DH ** -0.5)).astype(BF16)
    if transposed_kv:
        kt = _dot_nt(wkv_ref[0:SB_W, :], h)
        kt_ref[...] = kt
        ktb_ref[...] = kt.astype(BF16)
        vt = _dot_nt(wkv_ref[SB_W:2 * SB_W, :], h)
        vt_ref[...] = vt
        vtb_ref[...] = vt.astype(BF16)
    else:
        ck_ref[...] = _dot_nt(h, wkv_ref[0:SB_W, :])
        cv_ref[...] = _dot_nt(h, wkv_ref[SB_W:2 * SB_W, :])


def _in_proj_outs(n):
    return [jax.ShapeDtypeStruct((n, GDN_CONV_DIM), F32), jax.ShapeDtypeStruct((n, GDN_QK), F32),
            jax.ShapeDtypeStruct((n, 128), F32), jax.ShapeDtypeStruct((n, 3 * SC_DIM), F32),
            jax.ShapeDtypeStruct((n, SB_W), BF16)]


def _in_proj_t(x, g, w, wkv_t, kt_all, vt_all, layer, batch, tm):
    n, d = x.shape
    t_len = n // batch
    nt = t_len // tm
    row = lambda width: pl.BlockSpec((tm, width), lambda b, t: (b * nt + t, 0))
    stacked = pl.BlockSpec((None, None, SB_W, tm), lambda b, t: (layer, b, 0, t))
    per_layer = pl.BlockSpec((None, SB_W, tm), lambda b, t: (b, 0, t))
    any_spec = pl.BlockSpec(memory_space=pl.ANY)
    out_shape = _in_proj_outs(n) + [
        jax.ShapeDtypeStruct(kt_all.shape, F32), jax.ShapeDtypeStruct(vt_all.shape, F32),
        jax.ShapeDtypeStruct((batch, SB_W, t_len), BF16), jax.ShapeDtypeStruct((batch, SB_W, t_len), BF16)]
    return pl.pallas_call(
        functools.partial(_in_proj_kernel, transposed_kv=True),
        out_shape=tuple(out_shape),
        grid=(batch, nt),
        in_specs=[row(d), _const_spec((1, d)), _const_spec(w.shape), _const_spec(wkv_t.shape),
                  any_spec, any_spec],
        out_specs=(row(GDN_CONV_DIM), row(GDN_QK), row(128), row(3 * SC_DIM), row(SB_W),
                   stacked, stacked, per_layer, per_layer),
        input_output_aliases={4: 5, 5: 6},
        compiler_params=_params(2),
        name="in_proj",
    )(x, g, w, wkv_t, kt_all, vt_all)


def _in_proj(x, g, w, wkv_t, tm):
    n, d = x.shape
    row = lambda width: pl.BlockSpec((tm, width), lambda i: (i, 0))
    out_shape = _in_proj_outs(n) + [jax.ShapeDtypeStruct((n, SB_W), F32)] * 2
    return pl.pallas_call(
        functools.partial(_in_proj_kernel, transposed_kv=False),
        out_shape=tuple(out_shape),
        grid=(n // tm,),
        in_specs=[row(d), _const_spec((1, d)), _const_spec(w.shape), _const_spec(wkv_t.shape)],
        out_specs=(row(GDN_CONV_DIM), row(GDN_QK), row(128), row(3 * SC_DIM), row(SB_W),
                   row(SB_W), row(SB_W)),
        compiler_params=_params(1),
        name="in_proj",
    )(x, g, w, wkv_t)


def _split2(x):
    hi = x.astype(BF16)
    return hi, (x - hi.astype(F32)).astype(BF16)


def _dot2(a, b_exact):
    hi, lo = _split2(a)
    return _dot(hi, b_exact) + _dot(lo, b_exact)


def _dot2_left(a_exact, b):
    hi, lo = _split2(b)
    return _dot(a_exact, hi) + _dot(a_exact, lo)


def _gdn_kernel(qkv_ref, gz_ref, gab_ref, sbcx_ref, hg_ref, hs_ref, s0_ref,
                cw_ref, scw_ref, gp_ref, gng_ref, sng_ref,
                ya_ref, yb_ref, s_ref, tg_ref, ts_ref, carry_g, carry_s, *, tt, chunk):
    nh = GDN_HEADS
    sw = nh * chunk
    sec = max(sw, 128)
    n_chunks = tt // chunk

    @pl.when(pl.program_id(1) == 0)
    def _():
        carry_g[...] = hg_ref[...]
        carry_s[...] = hs_ref[...]
        s_ref[...] = s0_ref[...]

    x = qkv_ref[...]
    qn, kn, vv = [_gdn_front(x[:, GDN_QK * p:GDN_QK * (p + 1)], carry_g[:, :, GDN_QK * p:GDN_QK * (p + 1)],
                             cw_ref, p) for p in range(3)]
    carry_g[...] = _last_rows(x, 1)
    tg_ref[...] = _last_rows(x, 1)
    yb, pre_s = _short_conv(sbcx_ref[...], carry_s[...], scw_ref, sng_ref)
    yb_ref[...] = yb
    carry_s[...] = _last_rows(pre_s, 1)
    ts_ref[...] = _last_rows(pre_s, 1)

    slab = gab_ref[...]
    beta_s = _sigmoid(slab)
    g_s = -jnp.exp(gp_ref[0:1, :]) * _softplus(slab + gp_ref[1:2, :])
    ri = lax.broadcasted_iota(jnp.int32, (tt, tt), 0)
    ci = lax.broadcasted_iota(jnp.int32, (tt, tt), 1)
    tri = jnp.where(jnp.logical_and(ri // chunk == ci // chunk, ri >= ci), 1.0, 0.0).astype(BF16)
    gc = _dot2_left(tri, g_s)
    lane128 = lax.broadcasted_iota(jnp.int32, (1, 128), 1)
    slab2 = jnp.where(lane128 < nh, beta_s, gc)
    ew = 2 * GDN_QK + 2 * sec
    er = lax.broadcasted_iota(jnp.int32, (128, ew), 0)
    ec = lax.broadcasted_iota(jnp.int32, (128, ew), 1)
    src_row = jnp.where(ec < GDN_QK, ec // GDN_DK,
              jnp.where(ec < 2 * GDN_QK, nh + (ec - GDN_QK) // GDN_DK,
              jnp.where(ec < 2 * GDN_QK + sec, (ec - 2 * GDN_QK) // chunk,
                        nh + (ec - 2 * GDN_QK - sec) // chunk)))
    expand = jnp.where(er == src_row, 1.0, 0.0).astype(BF16)
    xp = _dot2(slab2, expand)
    b512 = xp[:, 0:GDN_QK]
    gc512 = xp[:, GDN_QK:2 * GDN_QK]
    b_s = xp[:, 2 * GDN_QK:2 * GDN_QK + sec][:, 0:sw]
    gc_s = xp[:, 2 * GDN_QK + sec:2 * GDN_QK + 2 * sec][:, 0:sw]
    eg512 = jnp.exp(gc512)

    qb = qn.astype(BF16)
    kb = kn.astype(BF16)
    qe = (qn * eg512).astype(BF16)
    rv = vv * b512
    rk = kn * (b512 * eg512)

    r_s = lax.broadcasted_iota(jnp.int32, (chunk, sw), 0)
    c_s = lax.broadcasted_iota(jnp.int32, (chunk, sw), 1)
    c_in = c_s - (c_s // chunk) * chunk
    causal_s = r_s >= c_in
    strict_s = r_s > c_in
    eye_s = jnp.where(r_s == c_in, 1.0, 0.0).astype(F32)
    pair_mask = jnp.logical_and(r_s // 2 == c_in // 2, strict_s)
    level_masks = []
    m = 2
    while m < chunk:
        level_masks.append(jnp.logical_and(
            jnp.logical_and(r_s // (2 * m) == c_in // (2 * m), r_s // m != c_in // m), strict_s))
        m *= 2
    bd_r = lax.broadcasted_iota(jnp.int32, (sw, sw), 0) // chunk
    bd_c = lax.broadcasted_iota(jnp.int32, (sw, sw), 1) // chunk
    bd_mask = bd_r == bd_c
    bdw_r = lax.broadcasted_iota(jnp.int32, (sw, GDN_QK), 0) // chunk
    bdw_c = lax.broadcasted_iota(jnp.int32, (sw, GDN_QK), 1) // GDN_DK
    bdw_mask = bdw_r == bdw_c
    zero_b = jnp.zeros((), BF16)

    def bd(a):
        return jnp.where(bd_mask, jnp.concatenate([a] * nh, axis=0), zero_b)

    def bdw(a):
        return jnp.where(bdw_mask, jnp.concatenate([a] * nh, axis=0), zero_b)

    tr = lax.broadcasted_iota(jnp.int32, (tt, sw), 0)
    tc = lax.broadcasted_iota(jnp.int32, (tt, sw), 1)
    eye_t = tr - (tr // chunk) * chunk == tc - (tc // chunk) * chunk
    same_chunk = jnp.where(ri // chunk == ci // chunk, 1.0, 0.0).astype(BF16)
    row_all = _dot2_left(same_chunk, jnp.where(eye_t, gc_s, 0.0))

    l_b, inv, qkd = [], [], []
    for c in range(n_chunks):
        rows = slice(c * chunk, (c + 1) * chunk)
        kq = _dot_nt(jnp.concatenate([kb[rows], qb[rows]], axis=0), bdw(kb[rows]))
        kk = kq[0:chunk]
        qk = kq[chunk:2 * chunk]
        decay = jnp.exp(jnp.where(causal_s, gc_s[rows] - row_all[rows], -jnp.inf))
        lm = jnp.where(strict_s, b_s[rows] * kk * decay, 0.0)
        l_b.append(lm.astype(BF16))
        inv.append(eye_s - jnp.where(pair_mask, lm, 0.0))
        qkd.append((qk * decay).astype(BF16))

    for mask in level_masks:
        for c in range(n_chunks):
            x_b = inv[c].astype(BF16)
            cx = _dot(jnp.where(mask, l_b[c], zero_b), bd(x_b))
            inv[c] = inv[c] - _dot(x_b, bd(cx.astype(BF16)))

    u_all, w_all = [], []
    for c in range(n_chunks):
        rows = slice(c * chunk, (c + 1) * chunk)
        x_b = inv[c].astype(BF16)
        u_all.append(_dot(x_b, bdw(rv[rows].astype(BF16))))
        w_all.append(_dot(x_b, bdw(rk[rows].astype(BF16))).astype(BF16))

    gng = gng_ref[...]
    for c in range(n_chunks):
        rows = slice(c * chunk, (c + 1) * chunk)
        g_last = gc512[(c + 1) * chunk - 1:(c + 1) * chunk, :]
        kd = (kn[rows] * jnp.exp(g_last - gc512[rows])).astype(BF16)
        eg_last = jnp.exp(g_last)
        s_old = [s_ref[h] for h in range(nh)]
        s_b = [s.astype(BF16) for s in s_old]
        wq = [_dot(jnp.concatenate([w_all[c][:, GDN_DK * h:GDN_DK * (h + 1)],
                                    qe[rows][:, GDN_DK * h:GDN_DK * (h + 1)]], axis=0), s_b[h])
              for h in range(nh)]
        v_new = u_all[c] - jnp.concatenate([wq[h][0:chunk] for h in range(nh)], axis=1)
        vb = v_new.astype(BF16)
        o_state = jnp.concatenate([wq[h][chunk:2 * chunk] for h in range(nh)], axis=1)
        o = o_state + _dot(qkd[c], bdw(vb))
        for h in range(nh):
            lanes = slice(GDN_DK * h, GDN_DK * (h + 1))
            s_ref[h] = s_old[h] * eg_last[:, lanes] + _dot_tn(kd[:, lanes], vb[:, lanes])
        z = gz_ref[c * chunk:(c + 1) * chunk, :]
        ya_ref[c * chunk:(c + 1) * chunk, :] = (jnp.concatenate(
            [_rms(o[:, GDN_DK * h:GDN_DK * (h + 1)], gng) for h in range(nh)], axis=1)
            * _silu(z)).astype(BF16)


def _gdn(qkv, gz, gab, sbcx, hist_g, hist_s, s0, cw, scw, gp, gng, sng, batch, tt, chunk):
    n = qkv.shape[0]
    nt = n // batch // tt
    row = lambda width: pl.BlockSpec((tt, width), lambda b, t: (b * nt + t, 0))
    per_b3 = lambda width: pl.BlockSpec((1, HIST, width), lambda b, t: (b, 0, 0))
    state = pl.BlockSpec((None, GDN_HEADS, GDN_DK, GDN_DK), lambda b, t: (b, 0, 0, 0))
    out_shape = (
        jax.ShapeDtypeStruct((n, GDN_QK), BF16),
        jax.ShapeDtypeStruct((n, SC_DIM), BF16),
        jax.ShapeDtypeStruct((batch, GDN_HEADS, GDN_DK, GDN_DK), F32),
        jax.ShapeDtypeStruct((batch, HIST, GDN_CONV_DIM), F32),
        jax.ShapeDtypeStruct((batch, HIST, SC_DIM), F32),
    )
    return pl.pallas_call(
        functools.partial(_gdn_kernel, tt=tt, chunk=chunk),
        out_shape=out_shape,
        grid=(batch, nt),
        in_specs=[row(GDN_CONV_DIM), row(GDN_QK), row(128), row(3 * SC_DIM),
                  per_b3(GDN_CONV_DIM), per_b3(SC_DIM), state,
                  _const_spec(cw.shape), _const_spec(scw.shape), _const_spec(gp.shape),
                  _const_spec(gng.shape), _const_spec(sng.shape)],
        out_specs=(row(GDN_QK), row(SC_DIM), state, per_b3(GDN_CONV_DIM), per_b3(SC_DIM)),
        scratch_shapes=[pltpu.VMEM((1, HIST, GDN_CONV_DIM), F32), pltpu.VMEM((1, HIST, SC_DIM), F32)],
        compiler_params=_params(2),
        name="gdn",
    )(qkv, gz, gab, sbcx, hist_g, hist_s, s0, cw, scw, gp, gng, sng)


def _sb_kernel(q_ref, kd_ref, vd_ref, kp_ref, vp_ref, g_ref, y_ref, acc_ref, c_ref,
               *, tq, tkd, tk, n_past_static):
    lane = lax.broadcasted_iota(jnp.int32, (1, SB_W), 1) // SB_DH
    rowh = lax.broadcasted_iota(jnp.int32, (SB_W, 1), 0) // SB_DH
    q = q_ref[...]
    zero_b = jnp.zeros((), BF16)
    q_heads = [jnp.where(lane == h, q, zero_b) for h in range(SB_HEADS)]

    acc_ref[...] = jnp.zeros_like(acc_ref)
    c_ref[...] = jnp.zeros_like(c_ref)

    def later_matrix(width):
        r = lax.broadcasted_iota(jnp.int32, (width, width), 0)
        c = lax.broadcasted_iota(jnp.int32, (width, width), 1)
        return jnp.where(r > c, 1.0, 0.0).astype(BF16)

    def block(kt, vt, later, mask):
        pv = jnp.zeros((tq, SB_W), F32)
        for h in range(SB_HEADS):
            z = _dot(q_heads[h], kt)
            sp = _softplus(z)
            spm = sp if mask is None else jnp.where(mask, sp, 0.0)
            rc = _dot2(spm, later)
            carry = c_ref[:, h:h + 1]
            a = jnp.exp((z - sp) - rc + carry)
            if mask is not None:
                a = jnp.where(mask, a, 0.0)
            pv = pv + _dot_nt(a.astype(BF16), jnp.where(rowh == h, vt, zero_b))
            c_ref[:, h:h + 1] = carry - (rc[:, 0:1] + spm[:, 0:1])
        acc_ref[...] += pv

    rq = lax.broadcasted_iota(jnp.int32, (tq, tkd), 0)
    ck = lax.broadcasted_iota(jnp.int32, (tq, tkd), 1)
    block(kd_ref[...], vd_ref[...], later_matrix(tkd), ck < rq)

    if n_past_static is None:
        n_past = pl.program_id(1) * (tq // tk)
    else:
        n_past = n_past_static
    later_past = later_matrix(tk)

    def cond(state):
        j, live = state
        return jnp.logical_and(j >= 0, live)

    def body(state):
        j, _ = state
        start = pl.multiple_of(j * tk, tk)
        kt = kp_ref[:, pl.ds(start, tk)].astype(BF16)
        vt = vp_ref[:, pl.ds(start, tk)].astype(BF16)
        block(kt, vt, later_past, None)
        live = jnp.max(c_ref[:, 0:SB_HEADS]) > SB_LOG_ZERO
        return j - 1, live

    lax.while_loop(cond, body, (n_past - 1, jnp.bool_(True)))

    o = acc_ref[...]
    ms = _dot2(o * o, _group_mean_matrix(SB_W, SB_DH))
    y_ref[...] = (o * lax.rsqrt(ms + EPS) * g_ref[...]).astype(BF16)


def _sb_attn(q, kd, vd, kp, vp, g, batch, tq, tkd, tk, n_past_static, layer=None):
    n = q.shape[0]
    nq = n // batch // tq
    qrow = pl.BlockSpec((tq, SB_W), lambda b, i: (b * nq + i, 0))
    if kd.shape[2] == tkd:
        diag = pl.BlockSpec((None, SB_W, tkd), lambda b, i: (b, 0, 0))
    else:
        diag = pl.BlockSpec((None, SB_W, tkd), lambda b, i: (b, 0, i))
    if layer is None:
        past = pl.BlockSpec((None, SB_W, kp.shape[2]), lambda b, i: (b, 0, 0))
    else:
        past = pl.BlockSpec((None, None, SB_W, kp.shape[3]), lambda b, i: (layer, b, 0, 0))
    return pl.pallas_call(
        functools.partial(_sb_kernel, tq=tq, tkd=tkd, tk=tk, n_past_static=n_past_static),
        out_shape=jax.ShapeDtypeStruct((n, SB_W), BF16),
        grid=(batch, nq),
        in_specs=[qrow, diag, diag, past, past, _const_spec(g.shape)],
        out_specs=qrow,
        scratch_shapes=[pltpu.VMEM((tq, SB_W), F32), pltpu.VMEM((tq, 128), F32)],
        compiler_params=_params(2),
        name="sb_attn",
    )(q, kd, vd, kp, vp, g)


def _mixmem_kernel(x_ref, ya_ref, yb_ref, yc_ref, wmix_ref, g_ref, wq_ref, qg_ref,
                   mk_ref, mv_ref, wo_ref, o_ref, *, n_seq):
    x1 = (x_ref[...]
          + _dot(ya_ref[...], wmix_ref[0:GDN_QK, :])
          + _dot(yb_ref[...], wmix_ref[GDN_QK:GDN_QK + SC_DIM, :])
          + _dot(yc_ref[...], wmix_ref[GDN_QK + SC_DIM:GDN_QK + SC_DIM + SB_W, :]))
    h = _rms(x1, g_ref[...]).astype(BF16)
    qm = _dot(h, wq_ref[...])
    qg = qg_ref[...]
    seg = qm.shape[0] // n_seq
    pairs = [(j, hd) for j in range(n_seq) for hd in range(MEM_HEADS)]
    lanes = lambda hd: slice(MEM_DH * hd, MEM_DH * (hd + 1))
    scores = [_dot_nt(_rms(qm[j * seg:(j + 1) * seg, lanes(hd)], qg).astype(BF16),
                      mk_ref[j, :, lanes(hd)].astype(BF16)) * (MEM_DH ** -0.5) for j, hd in pairs]
    probs = []
    for s in scores:
        e = jnp.exp(s - jnp.max(s, axis=-1, keepdims=True))
        probs.append((e / jnp.sum(e, axis=-1, keepdims=True)).astype(BF16))
    outs = [_dot(p, mv_ref[j, :, lanes(hd)].astype(BF16)).astype(BF16) for p, (j, hd) in zip(probs, pairs)]
    om = jnp.concatenate([jnp.concatenate(outs[j * MEM_HEADS:(j + 1) * MEM_HEADS], axis=1)
                          for j in range(n_seq)], axis=0)
    o_ref[...] = x1 + _dot(om, wo_ref[...])


def _mixmem(x, ya, yb, yc, wmix, g, wq, qg, mk, mv, wo, batch, tm, n_seq):
    n, d = x.shape
    nt = n // (batch // n_seq) // tm
    row = lambda width: pl.BlockSpec((tm, width), lambda b, t: (b * nt + t, 0))
    mem = pl.BlockSpec((n_seq,) + mk.shape[1:], lambda b, t: (b, 0, 0))
    return pl.pallas_call(
        functools.partial(_mixmem_kernel, n_seq=n_seq),
        out_shape=jax.ShapeDtypeStruct((n, d), F32),
        grid=(batch // n_seq, nt),
        in_specs=[row(d), row(GDN_QK), row(SC_DIM), row(SB_W),
                  _const_spec(wmix.shape), _const_spec(g.shape), _const_spec(wq.shape),
                  _const_spec(qg.shape), mem, mem, _const_spec(wo.shape)],
        out_specs=row(d),
        compiler_params=_params(2),
        name="mixmem",
    )(x, ya, yb, yc, wmix, g, wq, qg, mk, mv, wo)


def _ffn_kernel(x_ref, g_ref, wg_ref, wu_ref, cw_ref, wd_ref, hist_ref, o_ref, tail_ref, *, n_seq, d_ff):
    @pl.when(pl.program_id(1) == 0)
    def _():
        tail_ref[...] = hist_ref[...]

    x = x_ref[...]
    h = _rms(x, g_ref[...]).astype(BF16)
    acc = x
    for c0 in range(0, d_ff, FF_CHUNK):
        cols = slice(c0, min(c0 + FF_CHUNK, d_ff))
        gate = _dot(h, wg_ref[:, cols])
        hist = tail_ref[:, :, cols]
        conv = (_shift_rows(gate, hist, 2) * cw_ref[0:1, cols]
                + _shift_rows(gate, hist, 1) * cw_ref[1:2, cols] + gate * cw_ref[2:3, cols])
        tail_ref[:, :, cols] = _last_rows(gate, n_seq)
        up = _dot(h, wu_ref[:, cols])
        hid = (_silu(conv) * up).astype(BF16)
        acc = acc + _dot(hid, wd_ref[cols, :])
    o_ref[...] = acc


def _ffn(x, g, wg, wu, cw, wd, hist, batch, tm, n_seq):
    n, d = x.shape
    d_ff = wg.shape[1]
    nt = n // (batch // n_seq) // tm
    row = pl.BlockSpec((tm, d), lambda b, t: (b * nt + t, 0))
    per_b = pl.BlockSpec((n_seq, HIST, d_ff), lambda b, t: (b, 0, 0))
    return pl.pallas_call(
        functools.partial(_ffn_kernel, n_seq=n_seq, d_ff=d_ff),
        out_shape=(jax.ShapeDtypeStruct((n, d), F32),
                   jax.ShapeDtypeStruct((batch, HIST, d_ff), F32)),
        grid=(batch // n_seq, nt),
        in_specs=[row, _const_spec(g.shape), _const_spec(wg.shape), _const_spec(wu.shape),
                  _const_spec(cw.shape), _const_spec(wd.shape), per_b],
        out_specs=(row, per_b),
        compiler_params=_params(2),
        name="ffn",
    )(x, g, wg, wu, cw, wd, hist)


def _mem_kv_kernel(m_ref, g_ref, wk_ref, wv_ref, kg_ref, k_ref, v_ref):
    m = _rms(m_ref[...], g_ref[...]).astype(BF16)
    k = _dot(m, wk_ref[...])
    kg = kg_ref[...]
    for hd in range(MEM_HEADS):
        lanes = slice(MEM_DH * hd, MEM_DH * (hd + 1))
        k_ref[:, lanes] = _rms(k[:, lanes], kg)
    v_ref[...] = _dot(m, wv_ref[...])


def _mem_kv(mem, g, wk, wv, kg, tm):
    n, d = mem.shape
    w = wk.shape[1]
    row = lambda width: pl.BlockSpec((tm, width), lambda i: (i, 0))
    return pl.pallas_call(
        _mem_kv_kernel,
        out_shape=(jax.ShapeDtypeStruct((n, w), F32), jax.ShapeDtypeStruct((n, w), F32)),
        grid=(n // tm,),
        in_specs=[row(d), _const_spec(g.shape), _const_spec(wk.shape), _const_spec(wv.shape),
                  _const_spec(kg.shape)],
        out_specs=(row(w), row(w)),
        compiler_params=_params(1),
        name="mem_kv",
    )(mem, g, wk, wv, kg)


def _pad_hist(buf):
    return jnp.pad(buf, ((0, 0), (HIST - buf.shape[1], 0), (0, 0)))


def _largest_tile(total, cap):
    t = min(total, cap)
    while total % t:
        t //= 2
    return t


def _layer(x, batch, lw, mem_k, mem_v, s0, hist_g, hist_s, hist_f, layer,
           kv_stack=None, sb_cache=None):
    n = x.shape[0]
    t_len = n // batch
    chunk = min(CHUNK, t_len)
    tt = _largest_tile(t_len, 8 * chunk)
    tm = _largest_tile(t_len, 1024)

    if kv_stack is not None:
        qkv, gz, gab, sbcx, cq, kt_all, vt_all, ktb, vtb = _in_proj_t(
            x, lw["norm_mix_g"], lw["w_in"], lw["w_kv_t"], kv_stack[0], kv_stack[1], layer, batch, tm)
        kv_out = (kt_all, vt_all)
    else:
        qkv, gz, gab, sbcx, cq, ck, cv = _in_proj(
            x, lw["norm_mix_g"], lw["w_in"], lw["w_kv_t"], _largest_tile(n, 512))
        kv_out = (ck, cv)

    ya, yb, s_new, tail_g, tail_s = _gdn(
        qkv, gz, gab, sbcx, hist_g, hist_s, s0, lw["gdn_conv_w"], lw["sc_conv_w"],
        lw["gate_params"], lw["gdn_norm_g"], lw["sc_norm_g"], batch, tt, chunk)

    if kv_stack is not None:
        tq = _largest_tile(t_len, 256)
        yc = _sb_attn(cq, ktb, vtb, ktb, vtb, lw["sb_norm_g"], batch, tq, tq, tq, None)
    else:
        tkd = 128
        new_t = lambda a: jnp.pad(jnp.swapaxes(a.reshape(batch, t_len, SB_W), 1, 2).astype(BF16),
                                  ((0, 0), (0, 0), (0, tkd - t_len)))
        p_len = sb_cache[0].shape[3]
        tk = _largest_tile(p_len, 256)
        yc = _sb_attn(cq, new_t(ck), new_t(cv), sb_cache[0], sb_cache[1], lw["sb_norm_g"],
                      batch, t_len, tkd, tk, p_len // tk, layer=layer)

    n_seq = max(1, min(batch, 128 // t_len))
    while batch % n_seq:
        n_seq -= 1
    tm_seq = tm * n_seq
    x = _mixmem(x, ya, yb, yc, lw["w_mix_out"], lw["norm_mem_g"], lw["w_mq"], lw["mq_norm_g"],
                mem_k, mem_v, lw["w_mo"], batch, tm_seq, n_seq)
    x, tail_f = _ffn(x, lw["norm_ffn_g"], lw["w_gate"], lw["w_up"], lw["ffn_conv_w"],
                     lw["w_down"], hist_f, batch, tm_seq, n_seq)
    return x, s_new, tail_g[:, HIST - 3:], tail_s[:, HIST - 2:], kv_out, tail_f[:, HIST - 2:]


def kernel(x_prompt, x_sample, mem_prompt, state_gdn, cache_gdn_conv, cache_sc_conv, cache_sb_k, cache_sb_v, cache_mem_k, cache_mem_v, cache_ffn_conv, norm_mix_g, w_in, gdn_conv_w, gdn_A_log, gdn_dt_bias, gdn_norm_g, sc_conv_w, sc_norm_g, sb_norm_g, w_mix_out, norm_mem_g, mem_in_norm_g, w_mq, w_mk, w_mv, mq_norm_g, mk_norm_g, w_mo, norm_ffn_g, w_gate, w_up, ffn_conv_w, w_down):
    depth = w_in.shape[0]
    bp, tp, d = x_prompt.shape
    bs, ts, _ = x_sample.shape
    n_mem = mem_prompt.shape[1]
    d_ff = w_gate.shape[2]
    p_len = cache_sb_k.shape[2]
    n_gate = 2 * GDN_QK + 2 * GDN_QK
    n_kv = w_in.shape[2] - 2 * SB_W

    xp = x_prompt.reshape(bp * tp, d)
    xs = x_sample.reshape(bs * ts, d)
    mem = mem_prompt.reshape(bp * n_mem, d)
    zeros = lambda b, c: jnp.zeros((b, HIST, c), F32)
    to_t = lambda c: jnp.transpose(c, (0, 1, 3, 4, 2)).reshape(depth, bs, SB_W, p_len)
    sb_cache = (to_t(cache_sb_k), to_t(cache_sb_v))
    kv_stack = (jnp.zeros((depth, bp, SB_W, tp), F32), jnp.zeros((depth, bp, SB_W, tp), F32))

    outs = [[] for _ in range(14)]
    for l in range(depth):
        w = w_in[l]
        w_packed = jnp.concatenate(
            [w[:, :n_gate + 2 * GDN_HEADS],
             jnp.zeros((d, 128 - 2 * GDN_HEADS), w.dtype),
             w[:, n_gate + 2 * GDN_HEADS:n_kv]], axis=1).astype(BF16)
        gate_params = jnp.zeros((8, 128), F32)
        gate_params = gate_params.at[0, GDN_HEADS:2 * GDN_HEADS].set(gdn_A_log[l])
        gate_params = gate_params.at[1, GDN_HEADS:2 * GDN_HEADS].set(gdn_dt_bias[l])
        row = lambda a: a[l].reshape(1, -1)
        lw = dict(
            norm_mix_g=row(norm_mix_g), w_in=w_packed, w_kv_t=w[:, n_kv:].T.astype(BF16),
            gdn_conv_w=gdn_conv_w[l],
            gate_params=gate_params, gdn_norm_g=row(gdn_norm_g), sc_conv_w=sc_conv_w[l],
            sc_norm_g=row(sc_norm_g), sb_norm_g=row(sb_norm_g),
            w_mix_out=w_mix_out[l].astype(BF16), norm_mem_g=row(norm_mem_g),
            w_mq=w_mq[l].astype(BF16), mq_norm_g=row(mq_norm_g), w_mo=w_mo[l].astype(BF16),
            norm_ffn_g=row(norm_ffn_g), w_gate=w_gate[l].astype(BF16), w_up=w_up[l].astype(BF16),
            ffn_conv_w=ffn_conv_w[l], w_down=w_down[l].astype(BF16))

        mk, mv = _mem_kv(mem, row(mem_in_norm_g), w_mk[l].astype(BF16), w_mv[l].astype(BF16),
                         row(mk_norm_g), _largest_tile(bp * n_mem, 512))
        mk3 = mk.reshape(bp, n_mem, MEM_HEADS * MEM_DH)
        mv3 = mv.reshape(bp, n_mem, MEM_HEADS * MEM_DH)
        xp, s_n, gc_n, sc_n, kv_stack, fc_n = _layer(
            xp, bp, lw, mk3, mv3, jnp.zeros((bp, GDN_HEADS, GDN_DK, GDN_DK), F32),
            zeros(bp, GDN_CONV_DIM), zeros(bp, SC_DIM), zeros(bp, d_ff), l, kv_stack=kv_stack)
        for i, a in ((0, s_n), (1, gc_n), (2, sc_n), (5, mk.reshape(bp, n_mem, MEM_HEADS, MEM_DH)),
                     (6, mv.reshape(bp, n_mem, MEM_HEADS, MEM_DH)), (7, fc_n)):
            outs[i].append(a)

        xs, s_n, gc_n, sc_n, (k_n, v_n), fc_n = _layer(
            xs, bs, lw,
            cache_mem_k[l].reshape(bs, n_mem, MEM_HEADS * MEM_DH),
            cache_mem_v[l].reshape(bs, n_mem, MEM_HEADS * MEM_DH),
            state_gdn[l], _pad_hist(cache_gdn_conv[l]), _pad_hist(cache_sc_conv[l]),
            _pad_hist(cache_ffn_conv[l]), l, sb_cache=sb_cache)
        for i, a in enumerate((s_n, gc_n, sc_n, k_n.reshape(bs, ts, SB_HEADS, SB_DH),
                               v_n.reshape(bs, ts, SB_HEADS, SB_DH), fc_n)):
            outs[8 + i].append(a)

    from_t = lambda a: jnp.transpose(a.reshape(depth, bp, SB_HEADS, SB_DH, tp), (0, 1, 4, 2, 3))
    stacked = [jnp.stack(o) if o else None for o in outs]
    stacked[3], stacked[4] = from_t(kv_stack[0]), from_t(kv_stack[1])
    return (xp.reshape(bp, tp, d), xs.reshape(bs, ts, d)) + tuple(stacked)
```
